```python
import jax, jax.numpy as jnp
from jax import lax
import numpy as np

D_MODEL = 2048
BATCH = 1
SEQ = 8192
DEPTH = 1
DEC_BATCH = 128
DEC_SEQ = 8
PAST_LEN = 2048
PAGE_SIZE = 128

ATT_WIDTH = D_MODEL // 2
CONV_WIDTH = D_MODEL - ATT_WIDTH
HEAD_DIM = 64
N_HEADS = ATT_WIDTH // HEAD_DIM
CONV_GROUPS = CONV_WIDTH // HEAD_DIM
WINDOWS = (128, 512, 2048)
DILATIONS = (1, 4, 16)
MAX_WINDOW = max(WINDOWS)
CONV_K = 3
D_FF = 4 * D_MODEL
Q_BLOCK = 128
EPS = 1e-6
IN_SIZES = (ATT_WIDTH, ATT_WIDTH, ATT_WIDTH, CONV_WIDTH, CONV_WIDTH, CONV_WIDTH)
IN_COLS = sum(IN_SIZES)

kernel_name = "hymba_dilated_swa_shortconv_decode_step"


def _key_distances():
    return np.stack([np.arange(w // d + 1, dtype=np.int32) * d for w, d in zip(WINDOWS, DILATIONS)])


def _alibi_slopes():
    return 2.0 ** (-8.0 * jnp.arange(1, N_HEADS + 1, dtype=jnp.float32) / N_HEADS)


def rmsnorm(x, g):
    xf = x.astype(jnp.float32)
    y = xf * lax.rsqrt(jnp.mean(xf * xf, axis=-1, keepdims=True) + EPS)
    return (y * g.astype(jnp.float32)).astype(x.dtype)


def _attend_block(q_blk, qi_blk, k_seq, v_seq):
    dist = jnp.asarray(_key_distances())
    idx = qi_blk[None, :, None] - dist[:, None, :]
    valid = idx >= 0
    idx_c = jnp.maximum(idx, 0)
    kg = jnp.take(k_seq, idx_c, axis=1)
    vg = jnp.take(v_seq, idx_c, axis=1)
    s = jnp.einsum('bqhd,bpqjhd->bpqhj', q_blk, kg).astype(jnp.float32)
    bias = -(_alibi_slopes()[None, None, :, None] * dist.astype(jnp.float32)[:, None, None, :])
    s = jnp.where(valid[None, :, :, None, :], s + bias, -jnp.inf)
    lse = jax.nn.logsumexp(s, axis=-1)
    p = jnp.exp(s - lse[..., None])
    o = jnp.einsum('bpqhj,bpqjhd->bpqhd', p, vg.astype(jnp.float32))
    alpha = jax.nn.softmax(lse, axis=1)
    out = jnp.einsum('bpqh,bpqhd->bqhd', alpha, o)
    return out.astype(q_blk.dtype)


def dilated_attention(q, k_seq, v_seq, q_start, block):
    B, S, H, Dh = q.shape
    nblk = S // block
    qb = q.reshape(B, nblk, block, H, Dh).transpose(1, 0, 2, 3, 4)
    qi = (q_start + jnp.arange(S, dtype=jnp.int32)).reshape(nblk, block)
    out = lax.map(lambda a: _attend_block(a[0], a[1], k_seq, v_seq), (qb, qi))
    return out.transpose(1, 0, 2, 3, 4).reshape(B, S, H, Dh)


def hybrid_layer(x, k_past, v_past, conv_past, block,
                 norm1_g, w_in, q_norm_g, k_norm_g, conv_w, w_out, norm2_g, w_up, w_down):
    B, S, _ = x.shape
    h = rmsnorm(x, norm1_g)
    proj = h @ w_in
    splits = [int(c) for c in np.cumsum(IN_SIZES)[:-1]]
    q, k, v, gb, gc, xc = jnp.split(proj, splits, axis=-1)
    q = rmsnorm(q.reshape(B, S, N_HEADS, HEAD_DIM), q_norm_g) * (HEAD_DIM ** -0.5)
    k = rmsnorm(k.reshape(B, S, N_HEADS, HEAD_DIM), k_norm_g)
    v = v.reshape(B, S, N_HEADS, HEAD_DIM)
    k_seq = jnp.concatenate([k_past, k], axis=1)
    v_seq = jnp.concatenate([v_past, v], axis=1)
    att = dilated_attention(q, k_seq, v_seq, k_past.shape[1], block).reshape(B, S, ATT_WIDTH)
    u_seq = jnp.concatenate([conv_past, gc * xc], axis=1)
    conv = u_seq[:, 0:S] * conv_w[0]
    for i in range(1, CONV_K):
        conv = conv + u_seq[:, i:i + S] * conv_w[i]
    cz = gb * conv
    x1 = x + jnp.concatenate([att, cz], axis=-1) @ w_out
    h2 = rmsnorm(x1, norm2_g)
    y = x1 + jnp.square(jax.nn.relu(h2 @ w_up)) @ w_down
    return y, k, v, u_seq[:, -(CONV_K - 1):]


def setup_inputs(seed: int = 0) -> dict:
    key = jax.random.key(seed)
    ks = jax.random.split(key, 16)
    f32 = jnp.float32
    buf = min(MAX_WINDOW, PAST_LEN)

    def nrm(k, shape, scale):
        return jax.random.normal(k, shape, f32) * scale

    return {
        "x_prompt": nrm(ks[0], (BATCH, SEQ, D_MODEL), 1.0),
        "x_sample": nrm(ks[1], (DEC_BATCH, DEC_SEQ, D_MODEL), 1.0),
        "state_k": nrm(ks[2], (DEPTH, DEC_BATCH, buf, N_HEADS, HEAD_DIM), 1.0),
        "state_v": nrm(ks[3], (DEPTH, DEC_BATCH, buf, N_HEADS, HEAD_DIM), 1.0),
        "state_conv": nrm(ks[4], (DEPTH, DEC_BATCH, CONV_K - 1, CONV_WIDTH), 1.0),
        "norm1_g": 1.0 + nrm(ks[5], (DEPTH, D_MODEL), 0.02),
        "w_in": nrm(ks[6], (DEPTH, D_MODEL, IN_COLS), D_MODEL ** -0.5),
        "q_norm_g": 1.0 + nrm(ks[7], (DEPTH, HEAD_DIM), 0.02),
        "k_norm_g": 1.0 + nrm(ks[8], (DEPTH, HEAD_DIM), 0.02),
        "conv_w": nrm(ks[9], (DEPTH, CONV_K, CONV_WIDTH), CONV_K ** -0.5),
        "w_out": nrm(ks[10], (DEPTH, D_MODEL, D_MODEL), D_MODEL ** -0.5),
        "norm2_g": 1.0 + nrm(ks[11], (DEPTH, D_MODEL), 0.02),
        "w_up": nrm(ks[12], (DEPTH, D_MODEL, D_FF), D_MODEL ** -0.5),
        "w_down": nrm(ks[13], (DEPTH, D_FF, D_MODEL), D_FF ** -0.5),
    }


def reference(x_prompt, x_sample, state_k, state_v, state_conv,
              norm1_g, w_in, q_norm_g, k_norm_g, conv_w, w_out, norm2_g, w_up, w_down):
    yp, ys = x_prompt, x_sample
    Bp, Sp, _ = x_prompt.shape
    keep_p = min(MAX_WINDOW, Sp)
    kp_l, vp_l, cp_l, ks_l, vs_l, cs_l = [], [], [], [], [], []
    for l in range(DEPTH):
        w = (norm1_g[l], w_in[l], q_norm_g[l], k_norm_g[l], conv_w[l], w_out[l],
             norm2_g[l], w_up[l], w_down[l])
        empty_kv = jnp.zeros((Bp, 0, N_HEADS, HEAD_DIM), yp.dtype)
        zero_conv = jnp.zeros((Bp, CONV_K - 1, CONV_WIDTH), yp.dtype)
        yp, kp, vp, cp = hybrid_layer(yp, empty_kv, empty_kv, zero_conv, Q_BLOCK, *w)
        ys, ksn, vsn, csn = hybrid_layer(ys, state_k[l], state_v[l], state_conv[l], 1, *w)
        kp_l.append(kp[:, -keep_p:])
        vp_l.append(vp[:, -keep_p:])
        cp_l.append(cp)
        ks_l.append(ksn)
        vs_l.append(vsn)
        cs_l.append(csn)
    new_k_prompt = jnp.stack(kp_l)
    new_v_prompt = jnp.stack(vp_l)
    new_conv_prompt = jnp.stack(cp_l)
    new_k_sample = jnp.stack(ks_l)
    new_v_sample = jnp.stack(vs_l)
    new_conv_sample = jnp.stack(cs_l)
    return (yp, ys, new_k_prompt, new_v_prompt, new_conv_prompt, new_k_sample, new_v_sample, new_conv_sample)
```

```python
import functools

import numpy as np
import jax
import jax.numpy as jnp
from jax import lax
from jax.experimental import pallas as pl
from jax.experimental.pallas import tpu as pltpu

F32 = jnp.float32
BF16 = jnp.bfloat16

D_MODEL = 2048
ATT_WIDTH = 1024
CONV_WIDTH = 1024
HEAD_DIM = 64
N_HEADS = 16
WINDOWS = (128, 512, 2048)
DILATIONS = (1, 4, 16)
MAX_WINDOW = 2048
CONV_K = 3
D_FF = 4 * D_MODEL
EPS = 1e-6
IN_COLS = 6 * 1024

LANES = 128
Q_TILE = 128
SUPER = MAX_WINDOW
VMEM_LIMIT = 56 * 1024 * 1024


def _dot(a, b):
    return jnp.dot(a, b, preferred_element_type=F32)


def _dot_nt(a, b):
    return lax.dot_general(a, b, (((1,), (1,)), ((), ())), preferred_element_type=F32)


def _alibi_slopes():
    return 2.0 ** (-8.0 * np.arange(1, N_HEADS + 1, dtype=np.float64) / N_HEADS)


def _head_rmsnorm(p, gain_row, red, expand):
    ss = _dot((p * p).astype(BF16), red)
    inv = lax.rsqrt(ss * (1.0 / HEAD_DIM) + EPS)
    hi = inv.astype(BF16)
    lo = (inv - hi.astype(F32)).astype(BF16)
    inv_e = _dot(hi, expand) + _dot(lo, expand)
    return p * inv_e * gain_row


def _inproj_kernel(x_ref, g1_ref, w_ref, qg_ref, kg_ref, red_ref, exp_ref, cw_ref, past_ref,
                   q_ref, k_ref, v_ref, cz_ref, u_ref,
                   h_scr, gb_scr, c_scr, ubuf, *, tm, grouped):
    i = pl.program_id(0)
    j = pl.program_id(1)

    @pl.when(j == 0)
    def _():
        x = x_ref[...]
        ms = jnp.mean(x * x, axis=-1, keepdims=True)
        h_scr[...] = (x * lax.rsqrt(ms + EPS) * g1_ref[...]).astype(BF16)

    proj = _dot(h_scr[...], w_ref[...])

    @pl.when(j == 0)
    def _():
        q_ref[...] = _head_rmsnorm(proj, qg_ref[...], red_ref[...], exp_ref[...]) * (HEAD_DIM ** -0.5)

    @pl.when(j == 1)
    def _():
        k_ref[...] = _head_rmsnorm(proj, kg_ref[...], red_ref[...], exp_ref[...])

    @pl.when(j == 2)
    def _():
        v_ref[...] = proj

    @pl.when(j == 3)
    def _():
        gb_scr[...] = proj

    @pl.when(j == 4)
    def _():
        c_scr[...] = proj

    @pl.when(j == 5)
    def _():
        u = c_scr[...] * proj
        cw = cw_ref[...]
        if grouped:
            g = tm // 8
            u3 = u.reshape(g, 8, CONV_WIDTH)
            past = past_ref[...]
            p0 = past[:, 0:1, :]
            p1 = past[:, 1:2, :]
            tok = lax.broadcasted_iota(jnp.int32, u3.shape, 1)
            prev1 = jnp.where(tok == 0, p1, pltpu.roll(u3, 1, axis=1))
            prev2 = jnp.where(tok == 0, p0, jnp.where(tok == 1, p1, pltpu.roll(u3, 2, axis=1)))
            conv = cw[0:1][None] * prev2 + cw[1:2][None] * prev1 + cw[2:3][None] * u3
            cz_ref[...] = (gb_scr[...] * conv.reshape(tm, CONV_WIDTH)).astype(BF16)
            u_ref[...] = u
        else:
            @pl.when(i == 0)
            def _():
                ubuf[0:8, :] = jnp.zeros((8, CONV_WIDTH), F32)
                ubuf[6:8, :] = past_ref[...]

            ubuf[8:tm + 8, :] = u
            conv = cw[0:1] * ubuf[6:tm + 6, :] + cw[1:2] * ubuf[7:tm + 7, :] + cw[2:3] * u
            cz_ref[...] = (gb_scr[...] * conv).astype(BF16)
            tail = ubuf[tm:tm + 8, :]
            ubuf[0:8, :] = tail
            u_ref[...] = tail


def _inproj(x, past, g1, w_in, qg, kg, red, expand, cw, *, grouped, tm=512):
    m = x.shape[0]
    assert m % tm == 0
    nt = m // tm
    row = lambda i, j: (i, 0)
    const2 = lambda i, j: (0, 0)
    if grouped:
        past_spec = pl.BlockSpec((tm // 8, CONV_K - 1, CONV_WIDTH), lambda i, j: (i, 0, 0))
        u_shape = jax.ShapeDtypeStruct((m, CONV_WIDTH), F32)
        u_spec = pl.BlockSpec((tm, CONV_WIDTH), row)
    else:
        past_spec = pl.BlockSpec((CONV_K - 1, CONV_WIDTH), const2)
        u_shape = jax.ShapeDtypeStruct((8, CONV_WIDTH), F32)
        u_spec = pl.BlockSpec((8, CONV_WIDTH), const2)
    out_f32 = jax.ShapeDtypeStruct((m, ATT_WIDTH), F32)
    return pl.pallas_call(
        functools.partial(_inproj_kernel, tm=tm, grouped=grouped),
        grid=(nt, 6),
        in_specs=[
            pl.BlockSpec((tm, D_MODEL), row),
            pl.BlockSpec((1, D_MODEL), const2),
            pl.BlockSpec((D_MODEL, 1024), lambda i, j: (0, j)),
            pl.BlockSpec((1, ATT_WIDTH), const2),
            pl.BlockSpec((1, ATT_WIDTH), const2),
            pl.BlockSpec((ATT_WIDTH, LANES), const2),
            pl.BlockSpec((LANES, ATT_WIDTH), const2),
            pl.BlockSpec((CONV_K, CONV_WIDTH), const2),
            past_spec,
        ],
        out_specs=[
            pl.BlockSpec((tm, ATT_WIDTH), row),
            pl.BlockSpec((tm, ATT_WIDTH), row),
            pl.BlockSpec((tm, ATT_WIDTH), row),
            pl.BlockSpec((tm, CONV_WIDTH), row),
            u_spec,
        ],
        out_shape=[out_f32, out_f32, out_f32,
                   jax.ShapeDtypeStruct((m, CONV_WIDTH), BF16), u_shape],
        scratch_shapes=[
            pltpu.VMEM((tm, D_MODEL), BF16),
            pltpu.VMEM((tm, CONV_WIDTH), F32),
            pltpu.VMEM((tm, CONV_WIDTH), F32),
            pltpu.VMEM((tm + 8, CONV_WIDTH), F32),
        ],
        compiler_params=pltpu.CompilerParams(
            dimension_semantics=("arbitrary", "arbitrary"), vmem_limit_bytes=VMEM_LIMIT),
        name="inproj_grouped" if grouped else "inproj_seq",
    )(x, g1, w_in, qg, kg, red, expand, cw, past)


def _prompt_bias_table():
    qi = np.arange(Q_TILE)[:, None]
    kc = np.arange(2 * Q_TILE)[None, :]
    jdist = (Q_TILE + qi - kc).astype(np.float64)
    valid = (jdist >= 0) & (jdist <= Q_TILE)
    slopes = _alibi_slopes()
    tab = np.empty((2, len(DILATIONS), N_HEADS, Q_TILE, 2 * Q_TILE), np.float32)
    for f in range(2):
        ok = valid & (kc >= Q_TILE) if f else valid
        for p, d in enumerate(DILATIONS):
            for h in range(N_HEADS):
                tab[f, p, h] = np.where(ok, -slopes[h] * d * jdist, -np.inf)
    return tab


def _prompt_attn_kernel(q_ref, kc_ref, kp_ref, vc_ref, vp_ref, bias_ref, o_ref, o_scr, m_scr, l_scr):
    first = jnp.where(pl.program_id(0) == 0, 1, 0)
    lane = lax.broadcasted_iota(jnp.int32, (Q_TILE, LANES), 1)
    lo = lane < HEAD_DIM

    def attend(pat, variant, rows, kprev, kcur, vprev, vcur):
        q = q_ref[rows, :]
        k2 = jnp.concatenate([kprev, kcur], axis=0).astype(BF16)
        v2 = jnp.concatenate([vprev, vcur], axis=0).astype(BF16)
        parts = []
        for e in range(2):
            qe = jnp.where(lo if e == 0 else jnp.logical_not(lo), q, 0.0).astype(BF16)
            s = _dot_nt(qe, k2) + bias_ref[variant, pat, e]
            m = jnp.max(s, axis=1, keepdims=True)
            p = jnp.exp(s - m)
            l = jnp.sum(p, axis=1, keepdims=True)
            o = _dot(p.astype(BF16), v2)
            parts.append((o, m, l))
        (o0, m0, l0), (o1, m1, l1) = parts
        o_scr[pat, rows, :] = jnp.where(lo, o0, o1)
        m_scr[pat, rows, :] = jnp.where(lo, m0, m1)
        l_scr[pat, rows, :] = jnp.where(lo, l0, l1)

    def pat16(r, carry):
        rows = pl.ds(r, Q_TILE, stride=16)
        attend(2, first, rows, kp_ref[rows, :], kc_ref[rows, :], vp_ref[rows, :], vc_ref[rows, :])
        return carry

    lax.fori_loop(0, 16, pat16, 0)

    def pat4_first(r4, carry):
        rows = pl.ds(r4, Q_TILE, stride=4)
        prev = pl.ds(SUPER - 512 + r4, Q_TILE, stride=4)
        attend(1, first, rows, kp_ref[prev, :], kc_ref[rows, :], vp_ref[prev, :], vc_ref[rows, :])
        return carry

    lax.fori_loop(0, 4, pat4_first, 0)

    def pat4_rest(t, carry):
        start = 512 * (1 + t // 4) + t % 4
        rows = pl.ds(start, Q_TILE, stride=4)
        prev = pl.ds(start - 512, Q_TILE, stride=4)
        attend(1, 0, rows, kc_ref[prev, :], kc_ref[rows, :], vc_ref[prev, :], vc_ref[rows, :])
        return carry

    lax.fori_loop(0, 12, pat4_rest, 0)

    rows0 = pl.ds(0, Q_TILE)
    prev0 = pl.ds(SUPER - Q_TILE, Q_TILE)
    attend(0, first, rows0, kp_ref[prev0, :], kc_ref[rows0, :], vp_ref[prev0, :], vc_ref[rows0, :])

    def pat1_rest(n, carry):
        start = pl.multiple_of(n * Q_TILE, Q_TILE)
        rows = pl.ds(start, Q_TILE)
        prev = pl.ds(start - Q_TILE, Q_TILE)
        attend(0, 0, rows, kc_ref[prev, :], kc_ref[rows, :], vc_ref[prev, :], vc_ref[rows, :])
        return carry

    lax.fori_loop(1, SUPER // Q_TILE, pat1_rest, 0)

    def merge(c, carry):
        rows = pl.ds(pl.multiple_of(c * 256, 256), 256)
        ms = [m_scr[p, rows, :] for p in range(3)]
        mx = jnp.maximum(jnp.maximum(ms[0], ms[1]), ms[2])
        num = jnp.zeros((256, LANES), F32)
        den = jnp.zeros((256, LANES), F32)
        for p in range(3):
            w = jnp.exp(ms[p] - mx)
            num = num + w * o_scr[p, rows, :]
            den = den + w * l_scr[p, rows, :]
        o_ref[rows, :] = (num / den).astype(BF16)
        return carry

    lax.fori_loop(0, SUPER // 256, merge, 0)


def _prompt_attention(q, k, v):
    s = q.shape[0]
    assert s % SUPER == 0
    bias = jnp.asarray(_prompt_bias_table())
    cur = lambda b, p: (b, p)
    prev = lambda b, p: (jnp.maximum(b - 1, 0), p)
    blk = (SUPER, LANES)
    return pl.pallas_call(
        _prompt_attn_kernel,
        grid=(s // SUPER, ATT_WIDTH // LANES),
        in_specs=[
            pl.BlockSpec(blk, cur),
            pl.BlockSpec(blk, cur),
            pl.BlockSpec(blk, prev),
            pl.BlockSpec(blk, cur),
            pl.BlockSpec(blk, prev),
            pl.BlockSpec((2, 3, 2, Q_TILE, 2 * Q_TILE), lambda b, p: (0, 0, p, 0, 0)),
        ],
        out_specs=pl.BlockSpec(blk, cur),
        out_shape=jax.ShapeDtypeStruct((s, ATT_WIDTH), BF16),
        scratch_shapes=[pltpu.VMEM((3, SUPER, LANES), F32)] * 3,
        compiler_params=pltpu.CompilerParams(
            dimension_semantics=("arbitrary", "arbitrary"), vmem_limit_bytes=VMEM_LIMIT),
        name="prompt_attention",
    )(q, k, k, v, v, bias)


DEC_ROWS = 16


def _decode_bias_tables(past_len, dec_seq):
    dist_max = MAX_WINDOW
    mult = np.zeros(dist_max + 1, np.float64)
    for w, d in zip(WINDOWS, DILATIONS):
        mult[np.arange(w // d + 1) * d] += 1.0
    slopes = _alibi_slopes()

    def table(dist):
        ok = (dist >= 0) & (dist <= dist_max)
        dc = np.clip(dist, 0, dist_max)
        mu = np.where(ok, mult[dc], 0.0)
        with np.errstate(divide="ignore"):
            logm = np.where(mu > 0, np.log(np.maximum(mu, 1e-30)), -np.inf)
        return -slopes[:, None, None] * dc[None].astype(np.float64) + logm[None]

    qpos = past_len + np.arange(dec_seq)
    old = np.zeros((N_HEADS, DEC_ROWS, past_len), np.float32)
    old[:, :dec_seq] = table(qpos[:, None] - np.arange(past_len)[None, :])
    new = np.full((N_HEADS, DEC_ROWS, DEC_ROWS), -np.inf, np.float32)
    new[:, :dec_seq, :dec_seq] = table(qpos[:, None] - qpos[None, :])
    new[:, dec_seq:, :] = 0.0
    return old, new


def _decode_attn_kernel(q_ref, kn_ref, vn_ref, kt_ref, vt_ref, bias_ref, biasn_ref, o_ref):
    def head(h, carry):
        q = q_ref[h].astype(BF16)
        s = _dot(q, kt_ref[h].astype(BF16)) + bias_ref[h]
        sn = _dot_nt(q, kn_ref[h].astype(BF16)) + biasn_ref[h]
        m = jnp.maximum(jnp.max(s, axis=1, keepdims=True), jnp.max(sn, axis=1, keepdims=True))
        p = jnp.exp(s - m)
        pn = jnp.exp(sn - m)
        l = jnp.sum(p, axis=1, keepdims=True) + jnp.sum(pn, axis=1, keepdims=True)
        o = _dot_nt(p.astype(BF16), vt_ref[h].astype(BF16)) + _dot(pn.astype(BF16), vn_ref[h].astype(BF16))
        o_ref[h] = o / l
        return carry

    lax.fori_loop(0, N_HEADS, head, 0)


def _decode_attention(q, kn, vn, kt, vt):
    nb, _, _, past = kt.shape
    old, new = _decode_bias_tables(past, 8)
    small = pl.BlockSpec((None, N_HEADS, DEC_ROWS, HEAD_DIM), lambda b: (b, 0, 0, 0))
    big = pl.BlockSpec((None, N_HEADS, HEAD_DIM, past), lambda b: (b, 0, 0, 0))
    return pl.pallas_call(
        _decode_attn_kernel,
        grid=(nb,),
        in_specs=[small, small, small, big, big,
                  pl.BlockSpec((N_HEADS, DEC_ROWS, past), lambda b: (0, 0, 0)),
                  pl.BlockSpec((N_HEADS, DEC_ROWS, DEC_ROWS), lambda b: (0, 0, 0))],
        out_specs=small,
        out_shape=jax.ShapeDtypeStruct((nb, N_HEADS, DEC_ROWS, HEAD_DIM), F32),
        compiler_params=pltpu.CompilerParams(
            dimension_semantics=("arbitrary",), vmem_limit_bytes=VMEM_LIMIT),
        name="decode_attention",
    )(q, kn, vn, kt, vt, jnp.asarray(old), jnp.asarray(new))


def _outproj_kernel(x_ref, att_ref, cz_ref, wo_ref, g2_ref, x1_ref, h2_ref):
    y = (x_ref[...]
         + _dot(att_ref[...].astype(BF16), wo_ref[0:ATT_WIDTH, :])
         + _dot(cz_ref[...], wo_ref[ATT_WIDTH:D_MODEL, :]))
    x1_ref[...] = y
    ms = jnp.mean(y * y, axis=-1, keepdims=True)
    h2_ref[...] = (y * lax.rsqrt(ms + EPS) * g2_ref[...]).astype(BF16)


def _outproj(x, att, cz, w_out, g2, *, tm=256):
    m = x.shape[0]
    assert m % tm == 0
    row = lambda i: (i, 0)
    const = lambda i: (0, 0)
    return pl.pallas_call(
        _outproj_kernel,
        grid=(m // tm,),
        in_specs=[
            pl.BlockSpec((tm, D_MODEL), row),
            pl.BlockSpec((tm, ATT_WIDTH), row),
            pl.BlockSpec((tm, CONV_WIDTH), row),
            pl.BlockSpec((D_MODEL, D_MODEL), const),
            pl.BlockSpec((1, D_MODEL), const),
        ],
        out_specs=[pl.BlockSpec((tm, D_MODEL), row), pl.BlockSpec((tm, D_MODEL), row)],
        out_shape=[jax.ShapeDtypeStruct((m, D_MODEL), F32), jax.ShapeDtypeStruct((m, D_MODEL), BF16)],
        compiler_params=pltpu.CompilerParams(
            dimension_semantics=("arbitrary",), vmem_limit_bytes=VMEM_LIMIT),
        name="outproj",
    )(x, att, cz, w_out, g2)


def _ffn_kernel(x1_ref, h2_ref, wu_ref, wd_ref, y_ref):
    @pl.when(pl.program_id(1) == 0)
    def _():
        y_ref[...] = x1_ref[...]

    a = _dot(h2_ref[...], wu_ref[...])
    g = jnp.square(jnp.maximum(a, 0.0)).astype(BF16)
    y_ref[...] += _dot(g, wd_ref[...])


def _ffn(x1, h2, w_up, w_down, *, tm=512, tf=1024):
    m = x1.shape[0]
    assert m % tm == 0 and D_FF % tf == 0
    row = lambda i, f: (i, 0)
    return pl.pallas_call(
        _ffn_kernel,
        grid=(m // tm, D_FF // tf),
        in_specs=[
            pl.BlockSpec((tm, D_MODEL), row),
            pl.BlockSpec((tm, D_MODEL), row),
            pl.BlockSpec((D_MODEL, tf), lambda i, f: (0, f)),
            pl.BlockSpec((tf, D_MODEL), lambda i, f: (f, 0)),
        ],
        out_specs=pl.BlockSpec((tm, D_MODEL), row),
        out_shape=jax.ShapeDtypeStruct((m, D_MODEL), F32),
        compiler_params=pltpu.CompilerParams(
            dimension_semantics=("arbitrary", "arbitrary"), vmem_limit_bytes=VMEM_LIMIT),
        name="ffn",
    )(x1, h2, w_up, w_down)


def _head_sum_matrices():
    head_of_col = np.arange(ATT_WIDTH) // HEAD_DIM
    red = (head_of_col[:, None] == np.arange(LANES)[None, :]).astype(np.float32)
    return jnp.asarray(red, BF16), jnp.asarray(red.T, BF16)


def _to_heads(a, nb, t):
    a = a.reshape(nb, t, N_HEADS, HEAD_DIM).transpose(0, 2, 1, 3)
    return jnp.pad(a, ((0, 0), (0, 0), (0, DEC_ROWS - t), (0, 0)))


def kernel(x_prompt, x_sample, state_k, state_v, state_conv, norm1_g, w_in, q_norm_g, k_norm_g,
           conv_w, w_out, norm2_g, w_up, w_down):
    depth = w_in.shape[0]
    assert depth == 1
    bp, sp, _ = x_prompt.shape
    nb, t, _ = x_sample.shape
    past = state_k.shape[2]
    assert bp == 1 and t == 8 and past == MAX_WINDOW and sp >= MAX_WINDOW

    red, expand = _head_sum_matrices()
    g1 = norm1_g[0][None]
    g2 = norm2_g[0][None]
    qg = jnp.tile(q_norm_g[0], N_HEADS)[None]
    kg = jnp.tile(k_norm_g[0], N_HEADS)[None]
    cw = conv_w[0]
    w_in_b = w_in[0].astype(BF16)
    w_out_b = w_out[0].astype(BF16)
    w_up_b = w_up[0].astype(BF16)
    w_down_b = w_down[0].astype(BF16)

    xp = x_prompt[0]
    zero_conv = jnp.zeros((CONV_K - 1, CONV_WIDTH), F32)
    qp, kp, vp, czp, up = _inproj(xp, zero_conv, g1, w_in_b, qg, kg, red, expand, cw, grouped=False)
    attp = _prompt_attention(qp, kp, vp)
    x1p, h2p = _outproj(xp, attp, czp, w_out_b, g2)
    yp = _ffn(x1p, h2p, w_up_b, w_down_b)

    xs = x_sample.reshape(nb * t, D_MODEL)
    qs, ks, vs, czs, us = _inproj(xs, state_conv[0], g1, w_in_b, qg, kg, red, expand, cw, grouped=True)
    kt = jnp.transpose(state_k[0], (0, 2, 3, 1))
    vt = jnp.transpose(state_v[0], (0, 2, 3, 1))
    atts = _decode_attention(_to_heads(qs, nb, t), _to_heads(ks, nb, t), _to_heads(vs, nb, t), kt, vt)
    atts = atts[:, :, :t, :].transpose(0, 2, 1, 3).reshape(nb * t, ATT_WIDTH)
    x1s, h2s = _outproj(xs, atts, czs, w_out_b, g2)
    ys = _ffn(x1s, h2s, w_up_b, w_down_b)

    keep = min(MAX_WINDOW, sp)
    new_k_prompt = kp[sp - keep:].reshape(1, 1, keep, N_HEADS, HEAD_DIM)
    new_v_prompt = vp[sp - keep:].reshape(1, 1, keep, N_HEADS, HEAD_DIM)
    new_conv_prompt = up[8 - (CONV_K - 1):].reshape(1, 1, CONV_K - 1, CONV_WIDTH)
    new_k_sample = ks.reshape(1, nb, t, N_HEADS, HEAD_DIM)
    new_v_sample = vs.reshape(1, nb, t, N_HEADS, HEAD_DIM)
    new_conv_sample = us.reshape(nb, t, CONV_WIDTH)[:, t - (CONV_K - 1):][None]
    return (yp[None], ys.reshape(nb, t, D_MODEL), new_k_prompt, new_v_prompt, new_conv_prompt,
            new_k_sample, new_v_sample, new_conv_sample)
```

```python
import functools

import numpy as np
import jax
import jax.numpy as jnp
from jax import lax
from jax.experimental import pallas as pl
from jax.experimental.pallas import tpu as pltpu

F32 = jnp.float32
BF16 = jnp.bfloat16

D_MODEL = 2048
ATT_WIDTH = 1024
CONV_WIDTH = 1024
HEAD_DIM = 64
N_HEADS = 16
WINDOWS = (128, 512, 2048)
DILATIONS = (1, 4, 16)
MAX_WINDOW = 2048
CONV_K = 3
D_FF = 4 * D_MODEL
EPS = 1e-6
IN_COLS = 6 * 1024

LANES = 128
Q_TILE = 128
SUPER = MAX_WINDOW
VMEM_LIMIT = 56 * 1024 * 1024


def _dot(a, b):
    return jnp.dot(a, b, preferred_element_type=F32)


def _dot_nt(a, b):
    return lax.dot_general(a, b, (((1,), (1,)), ((), ())), preferred_element_type=F32)


def _alibi_slopes():
    return 2.0 ** (-8.0 * np.arange(1, N_HEADS + 1, dtype=np.float64) / N_HEADS)


def _head_rmsnorm(p, gain_row, red, expand):
    ss = _dot((p * p).astype(BF16), red)
    inv = lax.rsqrt(ss * (1.0 / HEAD_DIM) + EPS)
    hi = inv.astype(BF16)
    lo = (inv - hi.astype(F32)).astype(BF16)
    inv_e = _dot(hi, expand) + _dot(lo, expand)
    return p * inv_e * gain_row


def _inproj_kernel(x_ref, g1_ref, w_ref, qg_ref, kg_ref, red_ref, exp_ref, cw_ref, past_ref,
                   q_ref, k_ref, v_ref, cz_ref, u_ref,
                   h_scr, gb_scr, c_scr, ubuf, *, tm, grouped):
    i = pl.program_id(0)
    j = pl.program_id(1)

    @pl.when(j == 0)
    def _():
        x = x_ref[...]
        ms = jnp.mean(x * x, axis=-1, keepdims=True)
        h_scr[...] = (x * lax.rsqrt(ms + EPS) * g1_ref[...]).astype(BF16)

    proj = _dot(h_scr[...], w_ref[...])

    @pl.when(j == 0)
    def _():
        q_ref[...] = _head_rmsnorm(proj, qg_ref[...], red_ref[...], exp_ref[...]) * (HEAD_DIM ** -0.5)

    @pl.when(j == 1)
    def _():
        k_ref[...] = _head_rmsnorm(proj, kg_ref[...], red_ref[...], exp_ref[...])

    @pl.when(j == 2)
    def _():
        v_ref[...] = proj

    @pl.when(j == 3)
    def _():
        gb_scr[...] = proj

    @pl.when(j == 4)
    def _():
        c_scr[...] = proj

    @pl.when(j == 5)
    def _():
        u = c_scr[...] * proj
        cw = cw_ref[...]
        if grouped:
            g = tm // 8
            u3 = u.reshape(g, 8, CONV_WIDTH)
            past = past_ref[...]
            p0 = past[:, 0:1, :]
            p1 = past[:, 1:2, :]
            tok = lax.broadcasted_iota(jnp.int32, u3.shape, 1)
            prev1 = jnp.where(tok == 0, p1, pltpu.roll(u3, 1, axis=1))
            prev2 = jnp.where(tok == 0, p0, jnp.where(tok == 1, p1, pltpu.roll(u3, 2, axis=1)))
            conv = cw[0:1][None] * prev2 + cw[1:2][None] * prev1 + cw[2:3][None] * u3
            cz_ref[...] = (gb_scr[...] * conv.reshape(tm, CONV_WIDTH)).astype(BF16)
            u_ref[...] = u
        else:
            @pl.when(i == 0)
            def _():
                ubuf[0:8, :] = jnp.zeros((8, CONV_WIDTH), F32)
                ubuf[6:8, :] = past_ref[...]

            ubuf[8:tm + 8, :] = u
            conv = cw[0:1] * ubuf[6:tm + 6, :] + cw[1:2] * ubuf[7:tm + 7, :] + cw[2:3] * u
            cz_ref[...] = (gb_scr[...] * conv).astype(BF16)
            tail = ubuf[tm:tm + 8, :]
            ubuf[0:8, :] = tail
            u_ref[...] = tail


def _inproj(x, past, g1, w_in, qg, kg, red, expand, cw, *, grouped, tm=512):
    m = x.shape[0]
    assert m % tm == 0
    nt = m // tm
    row = lambda i, j: (i, 0)
    const2 = lambda i, j: (0, 0)
    if grouped:
        past_spec = pl.BlockSpec((tm // 8, CONV_K - 1, CONV_WIDTH), lambda i, j: (i, 0, 0))
        u_shape = jax.ShapeDtypeStruct((m, CONV_WIDTH), F32)
        u_spec = pl.BlockSpec((tm, CONV_WIDTH), row)
    else:
        past_spec = pl.BlockSpec((CONV_K - 1, CONV_WIDTH), const2)
        u_shape = jax.ShapeDtypeStruct((8, CONV_WIDTH), F32)
        u_spec = pl.BlockSpec((8, CONV_WIDTH), const2)
    out_f32 = jax.ShapeDtypeStruct((m, ATT_WIDTH), F32)
    return pl.pallas_call(
        functools.partial(_inproj_kernel, tm=tm, grouped=grouped),
        grid=(nt, 6),
        in_specs=[
            pl.BlockSpec((tm, D_MODEL), row),
            pl.BlockSpec((1, D_MODEL), const2),
            pl.BlockSpec((D_MODEL, 1024), lambda i, j: (0, j)),
            pl.BlockSpec((1, ATT_WIDTH), const2),
            pl.BlockSpec((1, ATT_WIDTH), const2),
            pl.BlockSpec((ATT_WIDTH, LANES), const2),
            pl.BlockSpec((LANES, ATT_WIDTH), const2),
            pl.BlockSpec((CONV_K, CONV_WIDTH), const2),
            past_spec,
        ],
        out_specs=[
            pl.BlockSpec((tm, ATT_WIDTH), row),
            pl.BlockSpec((tm, ATT_WIDTH), row),
            pl.BlockSpec((tm, ATT_WIDTH), row),
            pl.BlockSpec((tm, CONV_WIDTH), row),
            u_spec,
        ],
        out_shape=[out_f32, out_f32, out_f32,
                   jax.ShapeDtypeStruct((m, CONV_WIDTH), BF16), u_shape],
        scratch_shapes=[
            pltpu.VMEM((tm, D_MODEL), BF16),
            pltpu.VMEM((tm, CONV_WIDTH), F32),
            pltpu.VMEM((tm, CONV_WIDTH), F32),
            pltpu.VMEM((tm + 8, CONV_WIDTH), F32),
        ],
        compiler_params=pltpu.CompilerParams(
            dimension_semantics=("arbitrary", "arbitrary"), vmem_limit_bytes=VMEM_LIMIT),
        name="inproj_grouped" if grouped else "inproj_seq",
    )(x, g1, w_in, qg, kg, red, expand, cw, past)


def _prompt_bias_table():
    qi = np.arange(Q_TILE)[:, None]
    kc = np.arange(2 * Q_TILE)[None, :]
    jdist = (Q_TILE + qi - kc).astype(np.float64)
    valid = (jdist >= 0) & (jdist <= Q_TILE)
    slopes = _alibi_slopes()
    tab = np.empty((2, len(DILATIONS), N_HEADS, Q_TILE, 2 * Q_TILE), np.float32)
    for f in range(2):
        ok = valid & (kc >= Q_TILE) if f else valid
        for p, d in enumerate(DILATIONS):
            for h in range(N_HEADS):
                tab[f, p, h] = np.where(ok, -slopes[h] * d * jdist, -np.inf)
    return tab


def _prompt_attn_kernel(q_ref, kc_ref, kp_ref, vc_ref, vp_ref, bias_ref, o_ref, o_scr, m_scr, l_scr):
    first = jnp.where(pl.program_id(0) == 0, 1, 0)
    lane = lax.broadcasted_iota(jnp.int32, (Q_TILE, LANES), 1)
    lo = lane < HEAD_DIM
    hi = jnp.logical_not(lo)

    def attend(units):
        loaded = []
        for pat, variant, rows, kprev_ref, vprev_ref, prev in units:
            q = q_ref[rows, :]
            k2 = jnp.concatenate([kprev_ref[prev, :], kc_ref[rows, :]], axis=0).astype(BF16)
            v2 = jnp.concatenate([vprev_ref[prev, :], vc_ref[rows, :]], axis=0).astype(BF16)
            loaded.append((q, k2, v2))
        scores = []
        for (pat, variant, _, _, _, _), (q, k2, _) in zip(units, loaded):
            for e, sel in enumerate((lo, hi)):
                qe = jnp.where(sel, q, 0.0).astype(BF16)
                scores.append(_dot_nt(qe, k2) + bias_ref[variant, pat, e])
        stats = []
        for s in scores:
            m = jnp.max(s, axis=1, keepdims=True)
            p = jnp.exp(s - m)
            stats.append((m, jnp.sum(p, axis=1, keepdims=True), p.astype(BF16)))
        for u, (pat, _, rows, _, _, _) in enumerate(units):
            v2 = loaded[u][2]
            (m0, l0, p0), (m1, l1, p1) = stats[2 * u], stats[2 * u + 1]
            o_scr[pat, rows, :] = jnp.where(lo, _dot(p0, v2), _dot(p1, v2))
            m_scr[pat, rows, :] = jnp.where(lo, m0, m1)
            l_scr[pat, rows, :] = jnp.where(lo, l0, l1)

    def unit16(r):
        rows = pl.ds(r, Q_TILE, stride=16)
        return (2, first, rows, kp_ref, vp_ref, rows)

    def unit4(start, from_prev_block):
        rows = pl.ds(start, Q_TILE, stride=4)
        if from_prev_block:
            return (1, first, rows, kp_ref, vp_ref, pl.ds(SUPER - 512 + start, Q_TILE, stride=4))
        return (1, 0, rows, kc_ref, vc_ref, pl.ds(start - 512, Q_TILE, stride=4))

    def unit1(start, from_prev_block):
        rows = pl.ds(start, Q_TILE)
        if from_prev_block:
            return (0, first, rows, kp_ref, vp_ref, pl.ds(SUPER - Q_TILE, Q_TILE))
        return (0, 0, rows, kc_ref, vc_ref, pl.ds(start - Q_TILE, Q_TILE))

    def pat16(g, carry):
        attend([unit16(g * 4 + u) for u in range(4)])
        return carry

    lax.fori_loop(0, 4, pat16, 0)

    attend([unit4(r4, True) for r4 in range(4)])
    attend([unit1(0, True)])

    rest4 = [unit4(512 * c + r4, False) for c in range(1, SUPER // 512) for r4 in range(4)]
    attend(rest4[:6])
    attend(rest4[6:])

    def pat1(g, carry):
        attend([unit1(pl.multiple_of((1 + 5 * g + u) * Q_TILE, Q_TILE), False) for u in range(5)])
        return carry

    lax.fori_loop(0, (SUPER // Q_TILE - 1) // 5, pat1, 0)

    def merge(c, carry):
        rows = pl.ds(pl.multiple_of(c * 256, 256), 256)
        ms = [m_scr[p, rows, :] for p in range(3)]
        mx = jnp.maximum(jnp.maximum(ms[0], ms[1]), ms[2])
        num = jnp.zeros((256, LANES), F32)
        den = jnp.zeros((256, LANES), F32)
        for p in range(3):
            w = jnp.exp(ms[p] - mx)
            num = num + w * o_scr[p, rows, :]
            den = den + w * l_scr[p, rows, :]
        o_ref[rows, :] = (num / den).astype(BF16)
        return carry

    lax.fori_loop(0, SUPER // 256, merge, 0)


def _prompt_attention(q, k, v):
    s = q.shape[0]
    assert s % SUPER == 0
    bias = jnp.asarray(_prompt_bias_table())
    cur = lambda b, p: (b, p)
    prev = lambda b, p: (jnp.maximum(b - 1, 0), p)
    blk = (SUPER, LANES)
    return pl.pallas_call(
        _prompt_attn_kernel,
        grid=(s // SUPER, ATT_WIDTH // LANES),
        in_specs=[
            pl.BlockSpec(blk, cur),
            pl.BlockSpec(blk, cur),
            pl.BlockSpec(blk, prev),
            pl.BlockSpec(blk, cur),
            pl.BlockSpec(blk, prev),
            pl.BlockSpec((2, 3, 2, Q_TILE, 2 * Q_TILE), lambda b, p: (0, 0, p, 0, 0)),
        ],
        out_specs=pl.BlockSpec(blk, cur),
        out_shape=jax.ShapeDtypeStruct((s, ATT_WIDTH), BF16),
        scratch_shapes=[pltpu.VMEM((3, SUPER, LANES), F32)] * 3,
        compiler_params=pltpu.CompilerParams(
            dimension_semantics=("arbitrary", "arbitrary"), vmem_limit_bytes=VMEM_LIMIT),
        name="prompt_attention",
    )(q, k, k, v, v, bias)


DEC_ROWS = 16


def _decode_bias_tables(past_len, dec_seq):
    dist_max = MAX_WINDOW
    mult = np.zeros(dist_max + 1, np.float64)
    for w, d in zip(WINDOWS, DILATIONS):
        mult[np.arange(w // d + 1) * d] += 1.0
    slopes = _alibi_slopes()

    def table(dist):
        ok = (dist >= 0) & (dist <= dist_max)
        dc = np.clip(dist, 0, dist_max)
        mu = np.where(ok, mult[dc], 0.0)
        with np.errstate(divide="ignore"):
            logm = np.where(mu > 0, np.log(np.maximum(mu, 1e-30)), -np.inf)
        return -slopes[:, None, None] * dc[None].astype(np.float64) + logm[None]

    qpos = past_len + np.arange(dec_seq)
    old = np.zeros((N_HEADS, DEC_ROWS, past_len), np.float32)
    old[:, :dec_seq] = table(qpos[:, None] - np.arange(past_len)[None, :])
    new = np.full((N_HEADS, DEC_ROWS, DEC_ROWS), -np.inf, np.float32)
    new[:, :dec_seq, :dec_seq] = table(qpos[:, None] - qpos[None, :])
    new[:, dec_seq:, :] = 0.0
    return old, new


def _decode_attn_kernel(q_ref, kn_ref, vn_ref, kt_ref, vt_ref, bias_ref, biasn_ref, o_ref):
    def head(h, carry):
        q = q_ref[h].astype(BF16)
        s = _dot(q, kt_ref[h].astype(BF16)) + bias_ref[h]
        sn = _dot_nt(q, kn_ref[h].astype(BF16)) + biasn_ref[h]
        m = jnp.maximum(jnp.max(s, axis=1, keepdims=True), jnp.max(sn, axis=1, keepdims=True))
        p = jnp.exp(s - m)
        pn = jnp.exp(sn - m)
        l = jnp.sum(p, axis=1, keepdims=True) + jnp.sum(pn, axis=1, keepdims=True)
        o = _dot_nt(p.astype(BF16), vt_ref[h].astype(BF16)) + _dot(pn.astype(BF16), vn_ref[h].astype(BF16))
        o_ref[h] = o / l
        return carry

    lax.fori_loop(0, N_HEADS, head, 0)


def _decode_attention(q, kn, vn, kt, vt):
    nb, _, _, past = kt.shape
    old, new = _decode_bias_tables(past, 8)
    small = pl.BlockSpec((None, N_HEADS, DEC_ROWS, HEAD_DIM), lambda b: (b, 0, 0, 0))
    big = pl.BlockSpec((None, N_HEADS, HEAD_DIM, past), lambda b: (b, 0, 0, 0))
    return pl.pallas_call(
        _decode_attn_kernel,
        grid=(nb,),
        in_specs=[small, small, small, big, big,
                  pl.BlockSpec((N_HEADS, DEC_ROWS, past), lambda b: (0, 0, 0)),
                  pl.BlockSpec((N_HEADS, DEC_ROWS, DEC_ROWS), lambda b: (0, 0, 0))],
        out_specs=small,
        out_shape=jax.ShapeDtypeStruct((nb, N_HEADS, DEC_ROWS, HEAD_DIM), F32),
        compiler_params=pltpu.CompilerParams(
            dimension_semantics=("arbitrary",), vmem_limit_bytes=VMEM_LIMIT),
        name="decode_attention",
    )(q, kn, vn, kt, vt, jnp.asarray(old), jnp.asarray(new))


def _outproj_kernel(x_ref, att_ref, cz_ref, wo_ref, g2_ref, x1_ref, h2_ref):
    y = (x_ref[...]
         + _dot(att_ref[...].astype(BF16), wo_ref[0:ATT_WIDTH, :])
         + _dot(cz_ref[...], wo_ref[ATT_WIDTH:D_MODEL, :]))
    x1_ref[...] = y
    ms = jnp.mean(y * y, axis=-1, keepdims=True)
    h2_ref[...] = (y * lax.rsqrt(ms + EPS) * g2_ref[...]).astype(BF16)


def _outproj(x, att, cz, w_out, g2, *, tm=256):
    m = x.shape[0]
    assert m % tm == 0
    row = lambda i: (i, 0)
    const = lambda i: (0, 0)
    return pl.pallas_call(
        _outproj_kernel,
        grid=(m // tm,),
        in_specs=[
            pl.BlockSpec((tm, D_MODEL), row),
            pl.BlockSpec((tm, ATT_WIDTH), row),
            pl.BlockSpec((tm, CONV_WIDTH), row),
            pl.BlockSpec((D_MODEL, D_MODEL), const),
            pl.BlockSpec((1, D_MODEL), const),
        ],
        out_specs=[pl.BlockSpec((tm, D_MODEL), row), pl.BlockSpec((tm, D_MODEL), row)],
        out_shape=[jax.ShapeDtypeStruct((m, D_MODEL), F32), jax.ShapeDtypeStruct((m, D_MODEL), BF16)],
        compiler_params=pltpu.CompilerParams(
            dimension_semantics=("arbitrary",), vmem_limit_bytes=VMEM_LIMIT),
        name="outproj",
    )(x, att, cz, w_out, g2)


def _ffn_kernel(x1_ref, h2_ref, wu_ref, wd_ref, y_ref):
    @pl.when(pl.program_id(1) == 0)
    def _():
        y_ref[...] = x1_ref[...]

    a = _dot(h2_ref[...], wu_ref[...])
    g = jnp.square(jnp.maximum(a, 0.0)).astype(BF16)
    y_ref[...] += _dot(g, wd_ref[...])


def _ffn(x1, h2, w_up, w_down, *, tm=512, tf=1024):
    m = x1.shape[0]
    assert m % tm == 0 and D_FF % tf == 0
    row = lambda i, f: (i, 0)
    return pl.pallas_call(
        _ffn_kernel,
        grid=(m // tm, D_FF // tf),
        in_specs=[
            pl.BlockSpec((tm, D_MODEL), row),
            pl.BlockSpec((tm, D_MODEL), row),
            pl.BlockSpec((D_MODEL, tf), lambda i, f: (0, f)),
            pl.BlockSpec((tf, D_MODEL), lambda i, f: (f, 0)),
        ],
        out_specs=pl.BlockSpec((tm, D_MODEL), row),
        out_shape=jax.ShapeDtypeStruct((m, D_MODEL), F32),
        compiler_params=pltpu.CompilerParams(
            dimension_semantics=("arbitrary", "arbitrary"), vmem_limit_bytes=VMEM_LIMIT),
        name="ffn",
    )(x1, h2, w_up, w_down)


def _head_sum_matrices():
    head_of_col = np.arange(ATT_WIDTH) // HEAD_DIM
    red = (head_of_col[:, None] == np.arange(LANES)[None, :]).astype(np.float32)
    return jnp.asarray(red, BF16), jnp.asarray(red.T, BF16)


def _to_heads(a, nb, t):
    a = a.reshape(nb, t, N_HEADS, HEAD_DIM).transpose(0, 2, 1, 3)
    return jnp.pad(a, ((0, 0), (0, 0), (0, DEC_ROWS - t), (0, 0)))


def kernel(x_prompt, x_sample, state_k, state_v, state_conv, norm1_g, w_in, q_norm_g, k_norm_g,
           conv_w, w_out, norm2_g, w_up, w_down):
    depth = w_in.shape[0]
    assert depth == 1
    bp, sp, _ = x_prompt.shape
    nb, t, _ = x_sample.shape
    past = state_k.shape[2]
    assert bp == 1 and t == 8 and past == MAX_WINDOW and sp >= MAX_WINDOW

    red, expand = _head_sum_matrices()
    g1 = norm1_g[0][None]
    g2 = norm2_g[0][None]
    qg = jnp.tile(q_norm_g[0], N_HEADS)[None]
    kg = jnp.tile(k_norm_g[0], N_HEADS)[None]
    cw = conv_w[0]
    w_in_b = w_in[0].astype(BF16)
    w_out_b = w_out[0].astype(BF16)
    w_up_b = w_up[0].astype(BF16)
    w_down_b = w_down[0].astype(BF16)

    xp = x_prompt[0]
    zero_conv = jnp.zeros((CONV_K - 1, CONV_WIDTH), F32)
    qp, kp, vp, czp, up = _inproj(xp, zero_conv, g1, w_in_b, qg, kg, red, expand, cw, grouped=False)
    attp = _prompt_attention(qp, kp, vp)
    x1p, h2p = _outproj(xp, attp, czp, w_out_b, g2)
    yp = _ffn(x1p, h2p, w_up_b, w_down_b)

    xs = x_sample.reshape(nb * t, D_MODEL)
    qs, ks, vs, czs, us = _inproj(xs, state_conv[0], g1, w_in_b, qg, kg, red, expand, cw, grouped=True)
    kt = jnp.transpose(state_k[0], (0, 2, 3, 1))
    vt = jnp.transpose(state_v[0], (0, 2, 3, 1))
    atts = _decode_attention(_to_heads(qs, nb, t), _to_heads(ks, nb, t), _to_heads(vs, nb, t), kt, vt)
    atts = atts[:, :, :t, :].transpose(0, 2, 1, 3).reshape(nb * t, ATT_WIDTH)
    x1s, h2s = _outproj(xs, atts, czs, w_out_b, g2)
    ys = _ffn(x1s, h2s, w_up_b, w_down_b)

    keep = min(MAX_WINDOW, sp)
    new_k_prompt = kp[sp - keep:].reshape(1, 1, keep, N_HEADS, HEAD_DIM)
    new_v_prompt = vp[sp - keep:].reshape(1, 1, keep, N_HEADS, HEAD_DIM)
    new_conv_prompt = up[8 - (CONV_K - 1):].reshape(1, 1, CONV_K - 1, CONV_WIDTH)
    new_k_sample = ks.reshape(1, nb, t, N_HEADS, HEAD_DIM)
    new_v_sample = vs.reshape(1, nb, t, N_HEADS, HEAD_DIM)
    new_conv_sample = us.reshape(nb, t, CONV_WIDTH)[:, t - (CONV_K - 1):][None]
    return (yp[None], ys.reshape(nb, t, D_MODEL), new_k_prompt, new_v_prompt, new_conv_prompt,
            new_k_sample, new_v_sample, new_conv_sample)
```

```python
import functools

import numpy as np
import jax
import jax.numpy as jnp
from jax import lax
from jax.experimental import pallas as pl
from jax.experimental.pallas import tpu as pltpu

F32 = jnp.float32
BF16 = jnp.bfloat16

D_MODEL = 2048
ATT_WIDTH = 1024
CONV_WIDTH = 1024
HEAD_DIM = 64
N_HEADS = 16
WINDOWS = (128, 512, 2048)
DILATIONS = (1, 4, 16)
MAX_WINDOW = 2048
CONV_K = 3
D_FF = 4 * D_MODEL
EPS = 1e-6
IN_COLS = 6 * 1024

LANES = 128
Q_TILE = 128
SUPER = MAX_WINDOW
VMEM_LIMIT = 56 * 1024 * 1024
VMEM_LIMIT_HOST = 60 * 1024 * 1024


def _dot(a, b):
    return jnp.dot(a, b, preferred_element_type=F32)


def _dot_nt(a, b):
    return lax.dot_general(a, b, (((1,), (1,)), ((), ())), preferred_element_type=F32)


def _alibi_slopes():
    return 2.0 ** (-8.0 * np.arange(1, N_HEADS + 1, dtype=np.float64) / N_HEADS)


def _head_rmsnorm(p, gain_row, red, expand):
    ss = _dot((p * p).astype(BF16), red)
    inv = lax.rsqrt(ss * (1.0 / HEAD_DIM) + EPS)
    hi = inv.astype(BF16)
    lo = (inv - hi.astype(F32)).astype(BF16)
    inv_e = _dot(hi, expand) + _dot(lo, expand)
    return p * inv_e * gain_row


def _inproj_kernel(x_ref, g1_ref, w_ref, qg_ref, kg_ref, red_ref, exp_ref, cw_ref, past_ref,
                   q_ref, k_ref, v_ref, cz_ref, u_ref,
                   h_scr, gb_scr, c_scr, ubuf, *, tm, grouped):
    i = pl.program_id(0)
    j = pl.program_id(1)

    @pl.when(j == 0)
    def _():
        x = x_ref[...]
        ms = jnp.mean(x * x, axis=-1, keepdims=True)
        h_scr[...] = (x * lax.rsqrt(ms + EPS) * g1_ref[...]).astype(BF16)

    proj = _dot(h_scr[...], w_ref[...])

    @pl.when(j == 0)
    def _():
        q_ref[...] = _head_rmsnorm(proj, qg_ref[...], red_ref[...], exp_ref[...]) * (HEAD_DIM ** -0.5)

    @pl.when(j == 1)
    def _():
        k_ref[...] = _head_rmsnorm(proj, kg_ref[...], red_ref[...], exp_ref[...])

    @pl.when(j == 2)
    def _():
        v_ref[...] = proj

    @pl.when(j == 3)
    def _():
        gb_scr[...] = proj

    @pl.when(j == 4)
    def _():
        c_scr[...] = proj

    @pl.when(j == 5)
    def _():
        u = c_scr[...] * proj
        cw = cw_ref[...]
        if grouped:
            g = tm // 8
            u3 = u.reshape(g, 8, CONV_WIDTH)
            past = past_ref[...]
            p0 = past[:, 0:1, :]
            p1 = past[:, 1:2, :]
            tok = lax.broadcasted_iota(jnp.int32, u3.shape, 1)
            prev1 = jnp.where(tok == 0, p1, pltpu.roll(u3, 1, axis=1))
            prev2 = jnp.where(tok == 0, p0, jnp.where(tok == 1, p1, pltpu.roll(u3, 2, axis=1)))
            conv = cw[0:1][None] * prev2 + cw[1:2][None] * prev1 + cw[2:3][None] * u3
            cz_ref[...] = (gb_scr[...] * conv.reshape(tm, CONV_WIDTH)).astype(BF16)
            u_ref[...] = u
        else:
            @pl.when(i == 0)
            def _():
                ubuf[0:8, :] = jnp.zeros((8, CONV_WIDTH), F32)
                ubuf[6:8, :] = past_ref[...]

            ubuf[8:tm + 8, :] = u
            conv = cw[0:1] * ubuf[6:tm + 6, :] + cw[1:2] * ubuf[7:tm + 7, :] + cw[2:3] * u
            cz_ref[...] = (gb_scr[...] * conv).astype(BF16)
            tail = ubuf[tm:tm + 8, :]
            ubuf[0:8, :] = tail
            u_ref[...] = tail


def _inproj(x, past, g1, w_in, qg, kg, red, expand, cw, *, grouped, tm=512):
    m = x.shape[0]
    assert m % tm == 0
    nt = m // tm
    row = lambda i, j: (i, 0)
    const2 = lambda i, j: (0, 0)
    if grouped:
        past_spec = pl.BlockSpec((tm // 8, CONV_K - 1, CONV_WIDTH), lambda i, j: (i, 0, 0))
        u_shape = jax.ShapeDtypeStruct((m, CONV_WIDTH), F32)
        u_spec = pl.BlockSpec((tm, CONV_WIDTH), row)
    else:
        past_spec = pl.BlockSpec((CONV_K - 1, CONV_WIDTH), const2)
        u_shape = jax.ShapeDtypeStruct((8, CONV_WIDTH), F32)
        u_spec = pl.BlockSpec((8, CONV_WIDTH), const2)
    out_f32 = jax.ShapeDtypeStruct((m, ATT_WIDTH), F32)
    return pl.pallas_call(
        functools.partial(_inproj_kernel, tm=tm, grouped=grouped),
        grid=(nt, 6),
        in_specs=[
            pl.BlockSpec((tm, D_MODEL), row),
            pl.BlockSpec((1, D_MODEL), const2),
            pl.BlockSpec((D_MODEL, 1024), lambda i, j: (0, j)),
            pl.BlockSpec((1, ATT_WIDTH), const2),
            pl.BlockSpec((1, ATT_WIDTH), const2),
            pl.BlockSpec((ATT_WIDTH, LANES), const2),
            pl.BlockSpec((LANES, ATT_WIDTH), const2),
            pl.BlockSpec((CONV_K, CONV_WIDTH), const2),
            past_spec,
        ],
        out_specs=[
            pl.BlockSpec((tm, ATT_WIDTH), row),
            pl.BlockSpec((tm, ATT_WIDTH), row),
            pl.BlockSpec((tm, ATT_WIDTH), row),
            pl.BlockSpec((tm, CONV_WIDTH), row),
            u_spec,
        ],
        out_shape=[out_f32, out_f32, out_f32,
                   jax.ShapeDtypeStruct((m, CONV_WIDTH), BF16), u_shape],
        scratch_shapes=[
            pltpu.VMEM((tm, D_MODEL), BF16),
            pltpu.VMEM((tm, CONV_WIDTH), F32),
            pltpu.VMEM((tm, CONV_WIDTH), F32),
            pltpu.VMEM((tm + 8, CONV_WIDTH), F32),
        ],
        compiler_params=pltpu.CompilerParams(
            dimension_semantics=("arbitrary", "arbitrary"), vmem_limit_bytes=VMEM_LIMIT),
        name="inproj_grouped" if grouped else "inproj_seq",
    )(x, g1, w_in, qg, kg, red, expand, cw, past)


def _prompt_bias_table():
    qi = np.arange(Q_TILE)[:, None]
    kc = np.arange(2 * Q_TILE)[None, :]
    jdist = (Q_TILE + qi - kc).astype(np.float64)
    valid = (jdist >= 0) & (jdist <= Q_TILE)
    slopes = _alibi_slopes()
    tab = np.empty((2, len(DILATIONS), N_HEADS, Q_TILE, 2 * Q_TILE), np.float32)
    for f in range(2):
        ok = valid & (kc >= Q_TILE) if f else valid
        for p, d in enumerate(DILATIONS):
            for h in range(N_HEADS):
                tab[f, p, h] = np.where(ok, -slopes[h] * d * jdist, -np.inf)
    return tab


def _prompt_attn_kernel(q_ref, kc_ref, kp_ref, vc_ref, vp_ref, bias_ref,
                        dq_ref, dkn_ref, dvn_ref, dkt_ref, dvt_ref, dbias_ref, dbiasn_ref,
                        o_ref, do_ref, o_scr, m_scr, l_scr):
    first = jnp.where(pl.program_id(0) == 0, 1, 0)
    quarter = pl.program_id(2)
    lane = lax.broadcasted_iota(jnp.int32, (Q_TILE, LANES), 1)
    lo = lane < HEAD_DIM
    hi = jnp.logical_not(lo)

    def attend(units):
        def scores(unit):
            pat, variant, rows, kprev_ref, vprev_ref, prev = unit
            q = q_ref[rows, :]
            k2 = jnp.concatenate([kprev_ref[prev, :], kc_ref[rows, :]], axis=0).astype(BF16)
            return [_dot_nt(jnp.where(sel, q, 0.0).astype(BF16), k2) + bias_ref[variant, pat, e]
                    for e, sel in enumerate((lo, hi))]

        def softmax(ss):
            out = []
            for s in ss:
                m = jnp.max(s, axis=1, keepdims=True)
                p = jnp.exp(s - m)
                out.append((m, jnp.sum(p, axis=1, keepdims=True), p.astype(BF16)))
            return out

        def output(unit, stats):
            pat, _, rows, _, vprev_ref, prev = unit
            v2 = jnp.concatenate([vprev_ref[prev, :], vc_ref[rows, :]], axis=0).astype(BF16)
            (m0, l0, p0), (m1, l1, p1) = stats
            o_scr[pat, rows, :] = jnp.where(lo, _dot(p0, v2), _dot(p1, v2))
            m_scr[pat, rows, :] = jnp.where(lo, m0, m1)
            l_scr[pat, rows, :] = jnp.where(lo, l0, l1)

        all_scores = [scores(unit) for unit in units]
        all_stats = [softmax(ss) for ss in all_scores]
        for unit, stats in zip(units, all_stats):
            output(unit, stats)

    def unit16(r):
        rows = pl.ds(r, Q_TILE, stride=16)
        return (2, first, rows, kp_ref, vp_ref, rows)

    def unit4(start, from_prev_block):
        rows = pl.ds(start, Q_TILE, stride=4)
        if from_prev_block:
            return (1, first, rows, kp_ref, vp_ref, pl.ds(SUPER - 512 + start, Q_TILE, stride=4))
        return (1, 0, rows, kc_ref, vc_ref, pl.ds(start - 512, Q_TILE, stride=4))

    def unit1(start, from_prev_block):
        rows = pl.ds(start, Q_TILE)
        if from_prev_block:
            return (0, first, rows, kp_ref, vp_ref, pl.ds(SUPER - Q_TILE, Q_TILE))
        return (0, 0, rows, kc_ref, vc_ref, pl.ds(start - Q_TILE, Q_TILE))

    def pat16(g, carry):
        attend([unit16(g * 4 + u) for u in range(4)])
        return carry

    rest4 = [unit4(512 * c + r4, False) for c in range(1, SUPER // 512) for r4 in range(4)]

    def pat1(g, carry=0):
        attend([unit1(pl.multiple_of((1 + 5 * g + u) * Q_TILE, Q_TILE), False) for u in range(5)])
        return carry

    def merge(c, carry):
        rows = pl.ds(pl.multiple_of(c * 256, 256), 256)
        ms = [m_scr[p, rows, :] for p in range(3)]
        mx = jnp.maximum(jnp.maximum(ms[0], ms[1]), ms[2])
        num = jnp.zeros((256, LANES), F32)
        den = jnp.zeros((256, LANES), F32)
        for p in range(3):
            w = jnp.exp(ms[p] - mx)
            num = num + w * o_scr[p, rows, :]
            den = den + w * l_scr[p, rows, :]
        o_ref[rows, :] = (num / den).astype(BF16)
        return carry

    @pl.when(quarter == 0)
    def _():
        lax.fori_loop(0, 4, pat16, 0)

    @pl.when(quarter == 1)
    def _():
        attend([unit4(r4, True) for r4 in range(4)])
        attend([unit1(0, True)])
        attend(rest4[:6])

    @pl.when(quarter == 2)
    def _():
        attend(rest4[6:])
        pat1(0)

    @pl.when(quarter == 3)
    def _():
        lax.fori_loop(1, (SUPER // Q_TILE - 1) // 5, pat1, 0)
        lax.fori_loop(0, SUPER // 256, merge, 0)

    step = (pl.program_id(0) * pl.num_programs(1) + pl.program_id(1)) * pl.num_programs(2) + quarter
    head_base = (step % (N_HEADS // DEC_HEAD_GROUP)) * DEC_HEAD_GROUP
    scores = _decode_scores(head_base, dq_ref, dkn_ref, dkt_ref, dbias_ref, dbiasn_ref)
    _decode_outputs(scores, dvn_ref, dvt_ref, do_ref)


PROMPT_QUARTERS = 4


def _prompt_attention(q, k, v, decode, batch0, nbatch):
    s = q.shape[0]
    assert s % SUPER == 0
    bias = jnp.asarray(_prompt_bias_table())
    cur = lambda b, p, c: (b, p)
    prev = lambda b, p, c: (jnp.maximum(b - 1, 0), p)
    blk = (SUPER, LANES)
    npair = ATT_WIDTH // LANES
    grid = (s // SUPER, npair, PROMPT_QUARTERS)
    step_of = lambda b, p, c: (b * npair + p) * PROMPT_QUARTERS + c
    d_specs, d_operands, d_out_spec, d_out_shape = _decode_host_specs(
        decode, batch0, nbatch, grid[0] * grid[1] * grid[2], step_of)
    return pl.pallas_call(
        _prompt_attn_kernel,
        grid=grid,
        in_specs=[
            pl.BlockSpec(blk, cur),
            pl.BlockSpec(blk, cur),
            pl.BlockSpec(blk, prev),
            pl.BlockSpec(blk, cur),
            pl.BlockSpec(blk, prev),
            pl.BlockSpec((2, 3, 2, Q_TILE, 2 * Q_TILE), lambda b, p, c: (0, 0, p, 0, 0)),
        ] + d_specs,
        out_specs=[pl.BlockSpec(blk, cur), d_out_spec],
        out_shape=[jax.ShapeDtypeStruct((s, ATT_WIDTH), BF16), d_out_shape],
        scratch_shapes=[pltpu.VMEM((3, SUPER, LANES), F32)] * 3,
        compiler_params=pltpu.CompilerParams(
            dimension_semantics=("arbitrary", "arbitrary", "arbitrary"), vmem_limit_bytes=VMEM_LIMIT),
        name="prompt_attention_with_decode_attention",
    )(q, k, k, v, v, bias, *d_operands)


DEC_SEQ = 8
DEC_HEAD_GROUP = 8


def _decode_bias_tables(past_len):
    dist_max = MAX_WINDOW
    mult = np.zeros(dist_max + 1, np.float64)
    for w, d in zip(WINDOWS, DILATIONS):
        mult[np.arange(w // d + 1) * d] += 1.0
    slopes = _alibi_slopes()

    def table(dist):
        ok = (dist >= 0) & (dist <= dist_max)
        dc = np.clip(dist, 0, dist_max)
        mu = np.where(ok, mult[dc], 0.0)
        with np.errstate(divide="ignore"):
            logm = np.where(mu > 0, np.log(np.maximum(mu, 1e-30)), -np.inf)
        return (-slopes[:, None, None] * dc[None].astype(np.float64) + logm[None]).astype(np.float32)

    qpos = past_len + np.arange(DEC_SEQ)
    old = table(qpos[:, None] - np.arange(past_len)[None, :])
    new = np.full((N_HEADS, DEC_SEQ, 2 * DEC_SEQ), -np.inf, np.float32)
    new[:, :, :DEC_SEQ] = table(qpos[:, None] - qpos[None, :])
    return old, new


def _pad16(a):
    return jnp.concatenate([a, jnp.zeros_like(a)], axis=0).astype(BF16)


def _decode_scores(head_base, q_ref, kn_ref, kt_ref, bias_ref, biasn_ref):
    out = []
    for u in range(DEC_HEAD_GROUP):
        c = slice(u * HEAD_DIM, (u + 1) * HEAD_DIM)
        q = _pad16(q_ref[:, c])
        s = _dot(q, kt_ref[u].astype(BF16))[:DEC_SEQ] + bias_ref[head_base + u]
        sn = _dot_nt(q, _pad16(kn_ref[:, c]))[:DEC_SEQ] + biasn_ref[head_base + u]
        out.append((s, sn))
    return out


def _decode_outputs(scores, vn_ref, vt_ref, o_ref):
    probs = []
    for s, sn in scores:
        m = jnp.maximum(jnp.max(s, axis=1, keepdims=True), jnp.max(sn, axis=1, keepdims=True))
        probs.append((jnp.exp(s - m), jnp.exp(sn - m)))
    for u, (p, pn) in enumerate(probs):
        c = slice(u * HEAD_DIM, (u + 1) * HEAD_DIM)
        l = jnp.sum(p, axis=1, keepdims=True) + jnp.sum(pn, axis=1, keepdims=True)
        o = (_dot_nt(_pad16(p), vt_ref[u].astype(BF16))[:DEC_SEQ]
             + _dot(_pad16(pn), _pad16(vn_ref[:, c]))[:DEC_SEQ])
        o_ref[:, c] = o / l


def _decode_host_specs(decode, batch0, nbatch, nsteps, step_of):
    q, kn, vn, kt, vt = decode
    nb, _, _, past = kt.shape
    gpr = N_HEADS // DEC_HEAD_GROUP
    gw = DEC_HEAD_GROUP * HEAD_DIM
    assert q.shape == (nb, DEC_SEQ, ATT_WIDTH) and nsteps == nbatch * gpr and batch0 + nbatch <= nb
    old, new = _decode_bias_tables(past)
    small = pl.BlockSpec((None, DEC_SEQ, gw),
                         lambda *g: (batch0 + step_of(*g) // gpr, 0, step_of(*g) % gpr))
    big = pl.BlockSpec((None, DEC_HEAD_GROUP, HEAD_DIM, past),
                       lambda *g: (batch0 + step_of(*g) // gpr, step_of(*g) % gpr, 0, 0))
    const3 = lambda *g: (0, 0, 0)
    in_specs = [small, small, small, big, big,
                pl.BlockSpec((N_HEADS, DEC_SEQ, past), const3),
                pl.BlockSpec((N_HEADS, DEC_SEQ, 2 * DEC_SEQ), const3)]
    out_spec = pl.BlockSpec((None, DEC_SEQ, gw), lambda *g: (step_of(*g) // gpr, 0, step_of(*g) % gpr))
    out_shape = jax.ShapeDtypeStruct((nbatch, DEC_SEQ, ATT_WIDTH), F32)
    return in_specs, (q, kn, vn, kt, vt, jnp.asarray(old), jnp.asarray(new)), out_spec, out_shape


def _outproj_kernel(x_ref, att_ref, cz_ref, wo_ref, g2_ref, x1_ref, h2_ref):
    y = (x_ref[...]
         + _dot(att_ref[...].astype(BF16), wo_ref[0:ATT_WIDTH, :])
         + _dot(cz_ref[...], wo_ref[ATT_WIDTH:D_MODEL, :]))
    x1_ref[...] = y
    ms = jnp.mean(y * y, axis=-1, keepdims=True)
    h2_ref[...] = (y * lax.rsqrt(ms + EPS) * g2_ref[...]).astype(BF16)


def _outproj(x, att, cz, w_out, g2, *, tm=256):
    m = x.shape[0]
    assert m % tm == 0
    row = lambda i: (i, 0)
    const = lambda i: (0, 0)
    return pl.pallas_call(
        _outproj_kernel,
        grid=(m // tm,),
        in_specs=[
            pl.BlockSpec((tm, D_MODEL), row),
            pl.BlockSpec((tm, ATT_WIDTH), row),
            pl.BlockSpec((tm, CONV_WIDTH), row),
            pl.BlockSpec((D_MODEL, D_MODEL), const),
            pl.BlockSpec((1, D_MODEL), const),
        ],
        out_specs=[pl.BlockSpec((tm, D_MODEL), row), pl.BlockSpec((tm, D_MODEL), row)],
        out_shape=[jax.ShapeDtypeStruct((m, D_MODEL), F32), jax.ShapeDtypeStruct((m, D_MODEL), BF16)],
        compiler_params=pltpu.CompilerParams(
            dimension_semantics=("arbitrary",), vmem_limit_bytes=VMEM_LIMIT),
        name="outproj",
    )(x, att, cz, w_out, g2)


def _ffn_kernel(x1_ref, h2_ref, wu_ref, wd_ref, *rest, groups_per_row):
    y_ref = rest[-2] if groups_per_row else rest[-1]

    @pl.when(pl.program_id(1) == 0)
    def _():
        y_ref[...] = x1_ref[...]

    def ffn_rows(rows):
        a = _dot(h2_ref[rows, :], wu_ref[...])
        g = jnp.square(jnp.maximum(a, 0.0)).astype(BF16)
        y_ref[rows, :] += _dot(g, wd_ref[...])

    if not groups_per_row:
        ffn_rows(slice(None))
        return

    q_ref, kn_ref, vn_ref, kt_ref, vt_ref, bias_ref, biasn_ref, _, o_ref = rest
    step = pl.program_id(0) * pl.num_programs(1) + pl.program_id(1)
    head_base = (step % groups_per_row) * DEC_HEAD_GROUP
    half = h2_ref.shape[0] // 2
    scores = _decode_scores(head_base, q_ref, kn_ref, kt_ref, bias_ref, biasn_ref)
    ffn_rows(slice(0, half))
    _decode_outputs(scores, vn_ref, vt_ref, o_ref)
    ffn_rows(slice(half, 2 * half))


def _ffn(x1, h2, w_up, w_down, decode=None, batch0=0, nbatch=0, *, tm=512, tf=1024):
    m = x1.shape[0]
    assert m % tm == 0 and D_FF % tf == 0
    nf = D_FF // tf
    row = lambda i, f: (i, 0)
    once = dict(pipeline_mode=pl.Buffered(1)) if decode is not None else {}
    in_specs = [
        pl.BlockSpec((tm, D_MODEL), row, **once),
        pl.BlockSpec((tm, D_MODEL), row, **once),
        pl.BlockSpec((D_MODEL, tf), lambda i, f: (0, f)),
        pl.BlockSpec((tf, D_MODEL), lambda i, f: (f, 0)),
    ]
    y_spec = pl.BlockSpec((tm, D_MODEL), row)
    y_shape = jax.ShapeDtypeStruct((m, D_MODEL), F32)
    if decode is None:
        return pl.pallas_call(
            functools.partial(_ffn_kernel, groups_per_row=0),
            grid=(m // tm, nf), in_specs=in_specs, out_specs=y_spec, out_shape=y_shape,
            compiler_params=pltpu.CompilerParams(
                dimension_semantics=("arbitrary", "arbitrary"), vmem_limit_bytes=VMEM_LIMIT),
            name="ffn",
        )(x1, h2, w_up, w_down)

    d_specs, d_operands, d_out_spec, d_out_shape = _decode_host_specs(
        decode, batch0, nbatch, (m // tm) * nf, lambda i, f: i * nf + f)
    return pl.pallas_call(
        functools.partial(_ffn_kernel, groups_per_row=N_HEADS // DEC_HEAD_GROUP),
        grid=(m // tm, nf),
        in_specs=in_specs + d_specs,
        out_specs=[y_spec, d_out_spec],
        out_shape=[y_shape, d_out_shape],
        compiler_params=pltpu.CompilerParams(
            dimension_semantics=("arbitrary", "arbitrary"), vmem_limit_bytes=VMEM_LIMIT_HOST),
        name="ffn_with_decode_attention",
    )(x1, h2, w_up, w_down, *d_operands)


def _head_sum_matrices():
    head_of_col = np.arange(ATT_WIDTH) // HEAD_DIM
    red = (head_of_col[:, None] == np.arange(LANES)[None, :]).astype(np.float32)
    return jnp.asarray(red, BF16), jnp.asarray(red.T, BF16)


def kernel(x_prompt, x_sample, state_k, state_v, state_conv, norm1_g, w_in, q_norm_g, k_norm_g,
           conv_w, w_out, norm2_g, w_up, w_down):
    depth = w_in.shape[0]
    assert depth == 1
    bp, sp, _ = x_prompt.shape
    nb, t, _ = x_sample.shape
    past = state_k.shape[2]
    assert bp == 1 and t == 8 and past == MAX_WINDOW and sp >= MAX_WINDOW

    red, expand = _head_sum_matrices()
    g1 = norm1_g[0][None]
    g2 = norm2_g[0][None]
    qg = jnp.tile(q_norm_g[0], N_HEADS)[None]
    kg = jnp.tile(k_norm_g[0], N_HEADS)[None]
    cw = conv_w[0]
    w_in_b = w_in[0].astype(BF16)
    w_out_b = w_out[0].astype(BF16)
    w_up_b = w_up[0].astype(BF16)
    w_down_b = w_down[0].astype(BF16)

    xp = x_prompt[0]
    zero_conv = jnp.zeros((CONV_K - 1, CONV_WIDTH), F32)
    qp, kp, vp, czp, up = _inproj(xp, zero_conv, g1, w_in_b, qg, kg, red, expand, cw, grouped=False)

    xs = x_sample.reshape(nb * t, D_MODEL)
    qs, ks, vs, czs, us = _inproj(xs, state_conv[0], g1, w_in_b, qg, kg, red, expand, cw, grouped=True)
    kt = jnp.transpose(state_k[0], (0, 2, 3, 1))
    vt = jnp.transpose(state_v[0], (0, 2, 3, 1))
    per_batch = lambda a: a.reshape(nb, t, ATT_WIDTH)
    decode = (per_batch(qs), per_batch(ks), per_batch(vs), kt, vt)
    gpr = N_HEADS // DEC_HEAD_GROUP
    nb_attn = (sp // SUPER) * (ATT_WIDTH // LANES) * PROMPT_QUARTERS // gpr
    assert 0 < nb_attn < nb

    attp, atts_a = _prompt_attention(qp, kp, vp, decode, 0, nb_attn)
    x1p, h2p = _outproj(xp, attp, czp, w_out_b, g2)
    ffn_tm = 1024
    ffn_tf = D_FF * (sp // ffn_tm) // ((nb - nb_attn) * gpr)
    yp, atts_f = _ffn(x1p, h2p, w_up_b, w_down_b, decode, nb_attn, nb - nb_attn, tm=ffn_tm, tf=ffn_tf)
    atts = jnp.concatenate([atts_a, atts_f], axis=0).reshape(nb * t, ATT_WIDTH)
    x1s, h2s = _outproj(xs, atts, czs, w_out_b, g2)
    ys = _ffn(x1s, h2s, w_up_b, w_down_b)

    keep = min(MAX_WINDOW, sp)
    new_k_prompt = kp[sp - keep:].reshape(1, 1, keep, N_HEADS, HEAD_DIM)
    new_v_prompt = vp[sp - keep:].reshape(1, 1, keep, N_HEADS, HEAD_DIM)
    new_conv_prompt = up[8 - (CONV_K - 1):].reshape(1, 1, CONV_K - 1, CONV_WIDTH)
    new_k_sample = ks.reshape(1, nb, t, N_HEADS, HEAD_DIM)
    new_v_sample = vs.reshape(1, nb, t, N_HEADS, HEAD_DIM)
    new_conv_sample = us.reshape(nb, t, CONV_WIDTH)[:, t - (CONV_K - 1):][None]
    return (yp[None], ys.reshape(nb, t, D_MODEL), new_k_prompt, new_v_prompt, new_conv_prompt,
            new_k_sample, new_v_sample, new_conv_sample)
```

```python
import functools

import numpy as np
import jax
import jax.numpy as jnp
from jax import lax
from jax.experimental import pallas as pl
from jax.experimental.pallas import tpu as pltpu

F32 = jnp.float32
BF16 = jnp.bfloat16

D_MODEL = 2048
ATT_WIDTH = 1024
CONV_WIDTH = 1024
HEAD_DIM = 64
N_HEADS = 16
WINDOWS = (128, 512, 2048)
DILATIONS = (1, 4, 16)
MAX_WINDOW = 2048
CONV_K = 3
D_FF = 4 * D_MODEL
EPS = 1e-6
IN_COLS = 6 * 1024

LANES = 128
Q_TILE = 128
SUPER = MAX_WINDOW
VMEM_LIMIT = 56 * 1024 * 1024
VMEM_LIMIT_HOST = 60 * 1024 * 1024


def _dot(a, b):
    return jnp.dot(a, b, preferred_element_type=F32)


def _dot_nt(a, b):
    return lax.dot_general(a, b, (((1,), (1,)), ((), ())), preferred_element_type=F32)


def _alibi_slopes():
    return 2.0 ** (-8.0 * np.arange(1, N_HEADS + 1, dtype=np.float64) / N_HEADS)


def _head_rmsnorm(p, gain_row, red, expand):
    ss = _dot((p * p).astype(BF16), red)
    inv = lax.rsqrt(ss * (1.0 / HEAD_DIM) + EPS)
    hi = inv.astype(BF16)
    lo = (inv - hi.astype(F32)).astype(BF16)
    inv_e = _dot(hi, expand) + _dot(lo, expand)
    return p * inv_e * gain_row


def _inproj_kernel(*refs, tm, grouped, n_cast):
    x_ref, g1_ref, w_ref, qg_ref, kg_ref, red_ref, exp_ref, cw_ref, past_ref = refs[:9]
    cast_in = refs[9:9 + n_cast]
    q_ref, k_ref, v_ref, cz_ref, u_ref = refs[9 + n_cast:14 + n_cast]
    cast_out = refs[14 + n_cast:14 + 2 * n_cast]
    h_scr, gb_scr, c_scr, ubuf = refs[14 + 2 * n_cast:]
    i = pl.program_id(0)
    j = pl.program_id(1)

    @pl.when(j == 0)
    def _():
        x = x_ref[...]
        ms = jnp.mean(x * x, axis=-1, keepdims=True)
        h_scr[...] = (x * lax.rsqrt(ms + EPS) * g1_ref[...]).astype(BF16)

    proj = _dot(h_scr[...], w_ref[...])

    for src, dst in zip(cast_in, cast_out):
        dst[...] = src[...].astype(BF16)

    @pl.when(j == 0)
    def _():
        q_ref[...] = _head_rmsnorm(proj, qg_ref[...], red_ref[...], exp_ref[...]) * (HEAD_DIM ** -0.5)

    @pl.when(j == 1)
    def _():
        k_ref[...] = _head_rmsnorm(proj, kg_ref[...], red_ref[...], exp_ref[...])

    @pl.when(j == 2)
    def _():
        v_ref[...] = proj

    @pl.when(j == 3)
    def _():
        gb_scr[...] = proj

    @pl.when(j == 4)
    def _():
        c_scr[...] = proj

    @pl.when(j == 5)
    def _():
        u = c_scr[...] * proj
        cw = cw_ref[...]
        if grouped:
            g = tm // 8
            u3 = u.reshape(g, 8, CONV_WIDTH)
            past = past_ref[...]
            p0 = past[:, 0:1, :]
            p1 = past[:, 1:2, :]
            tok = lax.broadcasted_iota(jnp.int32, u3.shape, 1)
            prev1 = jnp.where(tok == 0, p1, pltpu.roll(u3, 1, axis=1))
            prev2 = jnp.where(tok == 0, p0, jnp.where(tok == 1, p1, pltpu.roll(u3, 2, axis=1)))
            conv = cw[0:1][None] * prev2 + cw[1:2][None] * prev1 + cw[2:3][None] * u3
            cz_ref[...] = (gb_scr[...] * conv.reshape(tm, CONV_WIDTH)).astype(BF16)
            u_ref[...] = u
        else:
            @pl.when(i == 0)
            def _():
                ubuf[0:8, :] = jnp.zeros((8, CONV_WIDTH), F32)
                ubuf[6:8, :] = past_ref[...]

            ubuf[8:tm + 8, :] = u
            conv = cw[0:1] * ubuf[6:tm + 6, :] + cw[1:2] * ubuf[7:tm + 7, :] + cw[2:3] * u
            cz_ref[...] = (gb_scr[...] * conv).astype(BF16)
            tail = ubuf[tm:tm + 8, :]
            ubuf[0:8, :] = tail
            u_ref[...] = tail


CAST_SLABS = 64


def _inproj(x, past, g1, w_in, qg, kg, red, expand, cw, *, grouped, casts=(), tm=512):
    m = x.shape[0]
    assert m % tm == 0
    nt = m // tm
    row = lambda i, j: (i, 0)
    const2 = lambda i, j: (0, 0)
    assert not casts or nt * 6 >= CAST_SLABS
    slab = lambda i, j: (jnp.minimum(i * 6 + j, CAST_SLABS - 1), 0)
    cast_specs = [pl.BlockSpec((a.shape[0] // CAST_SLABS, a.shape[1]), slab) for a in casts]
    cast_shapes = [jax.ShapeDtypeStruct(a.shape, BF16) for a in casts]
    assert all(a.shape[0] % (16 * CAST_SLABS) == 0 for a in casts)
    if grouped:
        past_spec = pl.BlockSpec((tm // 8, CONV_K - 1, CONV_WIDTH), lambda i, j: (i, 0, 0))
        u_shape = jax.ShapeDtypeStruct((m, CONV_WIDTH), F32)
        u_spec = pl.BlockSpec((tm, CONV_WIDTH), row)
    else:
        past_spec = pl.BlockSpec((CONV_K - 1, CONV_WIDTH), const2)
        u_shape = jax.ShapeDtypeStruct((8, CONV_WIDTH), F32)
        u_spec = pl.BlockSpec((8, CONV_WIDTH), const2)
    out_f32 = jax.ShapeDtypeStruct((m, ATT_WIDTH), F32)
    return pl.pallas_call(
        functools.partial(_inproj_kernel, tm=tm, grouped=grouped, n_cast=len(casts)),
        grid=(nt, 6),
        in_specs=[
            pl.BlockSpec((tm, D_MODEL), row),
            pl.BlockSpec((1, D_MODEL), const2),
            pl.BlockSpec((D_MODEL, 1024), lambda i, j: (0, j)),
            pl.BlockSpec((1, ATT_WIDTH), const2),
            pl.BlockSpec((1, ATT_WIDTH), const2),
            pl.BlockSpec((ATT_WIDTH, LANES), const2),
            pl.BlockSpec((LANES, ATT_WIDTH), const2),
            pl.BlockSpec((CONV_K, CONV_WIDTH), const2),
            past_spec,
        ] + cast_specs,
        out_specs=[
            pl.BlockSpec((tm, ATT_WIDTH), row),
            pl.BlockSpec((tm, ATT_WIDTH), row),
            pl.BlockSpec((tm, ATT_WIDTH), row),
            pl.BlockSpec((tm, CONV_WIDTH), row),
            u_spec,
        ] + cast_specs,
        out_shape=[out_f32, out_f32, out_f32,
                   jax.ShapeDtypeStruct((m, CONV_WIDTH), BF16), u_shape] + cast_shapes,
        scratch_shapes=[
            pltpu.VMEM((tm, D_MODEL), BF16),
            pltpu.VMEM((tm, CONV_WIDTH), F32),
            pltpu.VMEM((tm, CONV_WIDTH), F32),
            pltpu.VMEM((tm + 8, CONV_WIDTH), F32),
        ],
        compiler_params=pltpu.CompilerParams(
            dimension_semantics=("arbitrary", "arbitrary"), vmem_limit_bytes=VMEM_LIMIT),
        name="inproj_grouped" if grouped else "inproj_seq",
    )(x, g1, w_in, qg, kg, red, expand, cw, past, *casts)


def _prompt_bias_table():
    qi = np.arange(Q_TILE)[:, None]
    kc = np.arange(2 * Q_TILE)[None, :]
    jdist = (Q_TILE + qi - kc).astype(np.float64)
    valid = (jdist >= 0) & (jdist <= Q_TILE)
    slopes = _alibi_slopes()
    tab = np.empty((2, len(DILATIONS), N_HEADS, Q_TILE, 2 * Q_TILE), np.float32)
    for f in range(2):
        ok = valid & (kc >= Q_TILE) if f else valid
        for p, d in enumerate(DILATIONS):
            for h in range(N_HEADS):
                tab[f, p, h] = np.where(ok, -slopes[h] * d * jdist, -np.inf)
    return tab


def _prompt_attn_kernel(q_ref, kc_ref, kp_ref, vc_ref, vp_ref, bias_ref,
                        dq_ref, dkn_ref, dvn_ref, dkt_ref, dvt_ref, dbias_ref, dbiasn_ref,
                        o_ref, do_ref, o_scr, m_scr, l_scr):
    first = jnp.where(pl.program_id(0) == 0, 1, 0)
    quarter = pl.program_id(2)
    lane = lax.broadcasted_iota(jnp.int32, (Q_TILE, LANES), 1)
    lo = lane < HEAD_DIM
    hi = jnp.logical_not(lo)

    def attend(units):
        def scores(unit):
            pat, variant, rows, kprev_ref, vprev_ref, prev = unit
            q = q_ref[rows, :]
            k2 = jnp.concatenate([kprev_ref[prev, :], kc_ref[rows, :]], axis=0).astype(BF16)
            return [_dot_nt(jnp.where(sel, q, 0.0).astype(BF16), k2) + bias_ref[variant, pat, e]
                    for e, sel in enumerate((lo, hi))]

        def softmax(ss):
            out = []
            for s in ss:
                m = jnp.max(s, axis=1, keepdims=True)
                p = jnp.exp(s - m)
                out.append((m, jnp.sum(p, axis=1, keepdims=True), p.astype(BF16)))
            return out

        def output(unit, stats):
            pat, _, rows, _, vprev_ref, prev = unit
            v2 = jnp.concatenate([vprev_ref[prev, :], vc_ref[rows, :]], axis=0).astype(BF16)
            (m0, l0, p0), (m1, l1, p1) = stats
            o_scr[pat, rows, :] = jnp.where(lo, _dot(p0, v2), _dot(p1, v2))
            m_scr[pat, rows, :] = jnp.where(lo, m0, m1)
            l_scr[pat, rows, :] = jnp.where(lo, l0, l1)

        all_scores = [scores(unit) for unit in units]
        all_stats = [softmax(ss) for ss in all_scores]
        for unit, stats in zip(units, all_stats):
            output(unit, stats)

    def unit16(r):
        rows = pl.ds(r, Q_TILE, stride=16)
        return (2, first, rows, kp_ref, vp_ref, rows)

    def unit4(start, from_prev_block):
        rows = pl.ds(start, Q_TILE, stride=4)
        if from_prev_block:
            return (1, first, rows, kp_ref, vp_ref, pl.ds(SUPER - 512 + start, Q_TILE, stride=4))
        return (1, 0, rows, kc_ref, vc_ref, pl.ds(start - 512, Q_TILE, stride=4))

    def unit1(start, from_prev_block):
        rows = pl.ds(start, Q_TILE)
        if from_prev_block:
            return (0, first, rows, kp_ref, vp_ref, pl.ds(SUPER - Q_TILE, Q_TILE))
        return (0, 0, rows, kc_ref, vc_ref, pl.ds(start - Q_TILE, Q_TILE))

    def pat16(g):
        return [unit16(g * 4 + u) for u in range(4)]

    first4 = [unit4(r4, True) for r4 in range(4)]
    rest4 = [unit4(512 * c + r4, False) for c in range(1, SUPER // 512) for r4 in range(4)]

    def pat1(g):
        return [unit1((1 + 5 * g + u) * Q_TILE, False) for u in range(5)]

    def merge(c, carry):
        rows = pl.ds(pl.multiple_of(c * 256, 256), 256)
        ms = [m_scr[p, rows, :] for p in range(3)]
        mx = jnp.maximum(jnp.maximum(ms[0], ms[1]), ms[2])
        num = jnp.zeros((256, LANES), F32)
        den = jnp.zeros((256, LANES), F32)
        for p in range(3):
            w = jnp.exp(ms[p] - mx)
            num = num + w * o_scr[p, rows, :]
            den = den + w * l_scr[p, rows, :]
        o_ref[rows, :] = (num / den).astype(BF16)
        return carry

    step = (pl.program_id(0) * pl.num_programs(1) + pl.program_id(1)) * pl.num_programs(2) + quarter
    head_base = (step % (N_HEADS // DEC_HEAD_GROUP)) * DEC_HEAD_GROUP

    def quarter_body(groups_a, groups_b):
        scores = _decode_scores(head_base, dq_ref, dkn_ref, dkt_ref, dbias_ref, dbiasn_ref)
        for g in groups_a:
            attend(g)
        _decode_outputs(scores, dvn_ref, dvt_ref, do_ref)
        for g in groups_b:
            attend(g)

    @pl.when(quarter == 0)
    def _():
        quarter_body([pat16(0)], [pat16(1), pat1(0)])

    @pl.when(quarter == 1)
    def _():
        quarter_body([pat16(2)], [pat16(3), pat1(1)])

    @pl.when(quarter == 2)
    def _():
        quarter_body([first4 + [unit1(0, True)], rest4[:6]], [rest4[6:]])

    @pl.when(quarter == 3)
    def _():
        quarter_body([pat1(2)], [])
        lax.fori_loop(0, SUPER // 256, merge, 0)


PROMPT_QUARTERS = 4


def _prompt_attention(q, k, v, decode, batch0, nbatch):
    s = q.shape[0]
    assert s % SUPER == 0
    bias = jnp.asarray(_prompt_bias_table())
    cur = lambda b, p, c: (b, p)
    prev = lambda b, p, c: (jnp.maximum(b - 1, 0), p)
    blk = (SUPER, LANES)
    npair = ATT_WIDTH // LANES
    grid = (s // SUPER, npair, PROMPT_QUARTERS)
    step_of = lambda b, p, c: (b * npair + p) * PROMPT_QUARTERS + c
    d_specs, d_operands, d_out_spec, d_out_shape = _decode_host_specs(
        decode, batch0, nbatch, grid[0] * grid[1] * grid[2], step_of)
    return pl.pallas_call(
        _prompt_attn_kernel,
        grid=grid,
        in_specs=[
            pl.BlockSpec(blk, cur),
            pl.BlockSpec(blk, cur),
            pl.BlockSpec(blk, prev),
            pl.BlockSpec(blk, cur),
            pl.BlockSpec(blk, prev),
            pl.BlockSpec((2, 3, 2, Q_TILE, 2 * Q_TILE), lambda b, p, c: (0, 0, p, 0, 0)),
        ] + d_specs,
        out_specs=[pl.BlockSpec(blk, cur), d_out_spec],
        out_shape=[jax.ShapeDtypeStruct((s, ATT_WIDTH), BF16), d_out_shape],
        scratch_shapes=[pltpu.VMEM((3, SUPER, LANES), F32)] * 3,
        compiler_params=pltpu.CompilerParams(
            dimension_semantics=("arbitrary", "arbitrary", "arbitrary"), vmem_limit_bytes=VMEM_LIMIT),
        name="prompt_attention_with_decode_attention",
    )(q, k, k, v, v, bias, *d_operands)


DEC_SEQ = 8
DEC_HEAD_GROUP = 8


def _decode_bias_tables(past_len):
    dist_max = MAX_WINDOW
    mult = np.zeros(dist_max + 1, np.float64)
    for w, d in zip(WINDOWS, DILATIONS):
        mult[np.arange(w // d + 1) * d] += 1.0
    slopes = _alibi_slopes()

    def table(dist):
        ok = (dist >= 0) & (dist <= dist_max)
        dc = np.clip(dist, 0, dist_max)
        mu = np.where(ok, mult[dc], 0.0)
        with np.errstate(divide="ignore"):
            logm = np.where(mu > 0, np.log(np.maximum(mu, 1e-30)), -np.inf)
        return (-slopes[:, None, None] * dc[None].astype(np.float64) + logm[None]).astype(np.float32)

    qpos = past_len + np.arange(DEC_SEQ)
    old = table(qpos[:, None] - np.arange(past_len)[None, :])
    new = np.full((N_HEADS, DEC_SEQ, 2 * DEC_SEQ), -np.inf, np.float32)
    new[:, :, :DEC_SEQ] = table(qpos[:, None] - qpos[None, :])
    return old, new


def _pad16(a):
    return jnp.concatenate([a, jnp.zeros_like(a)], axis=0).astype(BF16)


def _decode_scores(head_base, q_ref, kn_ref, kt_ref, bias_ref, biasn_ref):
    out = []
    for u in range(DEC_HEAD_GROUP):
        c = slice(u * HEAD_DIM, (u + 1) * HEAD_DIM)
        q = _pad16(q_ref[:, c])
        s = _dot(q, kt_ref[u].astype(BF16))[:DEC_SEQ] + bias_ref[head_base + u]
        sn = _dot_nt(q, _pad16(kn_ref[:, c]))[:DEC_SEQ] + biasn_ref[head_base + u]
        out.append((s, sn))
    return out


def _decode_outputs(scores, vn_ref, vt_ref, o_ref):
    probs = []
    for s, sn in scores:
        m = jnp.maximum(jnp.max(s, axis=1, keepdims=True), jnp.max(sn, axis=1, keepdims=True))
        probs.append((jnp.exp(s - m), jnp.exp(sn - m)))
    for u, (p, pn) in enumerate(probs):
        c = slice(u * HEAD_DIM, (u + 1) * HEAD_DIM)
        l = jnp.sum(p, axis=1, keepdims=True) + jnp.sum(pn, axis=1, keepdims=True)
        o = (_dot_nt(_pad16(p), vt_ref[u].astype(BF16))[:DEC_SEQ]
             + _dot(_pad16(pn), _pad16(vn_ref[:, c]))[:DEC_SEQ])
        o_ref[:, c] = o / l


def _decode_host_specs(decode, batch0, nbatch, nsteps, step_of):
    q, kn, vn, kt, vt = decode
    nb, _, _, past = kt.shape
    gpr = N_HEADS // DEC_HEAD_GROUP
    gw = DEC_HEAD_GROUP * HEAD_DIM
    assert q.shape == (nb, DEC_SEQ, ATT_WIDTH) and nsteps == nbatch * gpr and batch0 + nbatch <= nb
    old, new = _decode_bias_tables(past)
    small = pl.BlockSpec((None, DEC_SEQ, gw),
                         lambda *g: (batch0 + step_of(*g) // gpr, 0, step_of(*g) % gpr))
    big = pl.BlockSpec((None, DEC_HEAD_GROUP, HEAD_DIM, past),
                       lambda *g: (batch0 + step_of(*g) // gpr, step_of(*g) % gpr, 0, 0))
    const3 = lambda *g: (0, 0, 0)
    in_specs = [small, small, small, big, big,
                pl.BlockSpec((N_HEADS, DEC_SEQ, past), const3),
                pl.BlockSpec((N_HEADS, DEC_SEQ, 2 * DEC_SEQ), const3)]
    out_spec = pl.BlockSpec((None, DEC_SEQ, gw), lambda *g: (step_of(*g) // gpr, 0, step_of(*g) % gpr))
    out_shape = jax.ShapeDtypeStruct((nbatch, DEC_SEQ, ATT_WIDTH), F32)
    return in_specs, (q, kn, vn, kt, vt, jnp.asarray(old), jnp.asarray(new)), out_spec, out_shape


def _outproj_kernel(x_ref, att_ref, cz_ref, wo_ref, g2_ref, x1_ref, h2_ref):
    y = (x_ref[...]
         + _dot(att_ref[...].astype(BF16), wo_ref[0:ATT_WIDTH, :])
         + _dot(cz_ref[...], wo_ref[ATT_WIDTH:D_MODEL, :]))
    x1_ref[...] = y
    ms = jnp.mean(y * y, axis=-1, keepdims=True)
    h2_ref[...] = (y * lax.rsqrt(ms + EPS) * g2_ref[...]).astype(BF16)


def _outproj(x, att, cz, w_out, g2, *, tm=512):
    m = x.shape[0]
    assert m % tm == 0
    row = lambda i: (i, 0)
    const = lambda i: (0, 0)
    return pl.pallas_call(
        _outproj_kernel,
        grid=(m // tm,),
        in_specs=[
            pl.BlockSpec((tm, D_MODEL), row),
            pl.BlockSpec((tm, ATT_WIDTH), row),
            pl.BlockSpec((tm, CONV_WIDTH), row),
            pl.BlockSpec((D_MODEL, D_MODEL), const),
            pl.BlockSpec((1, D_MODEL), const),
        ],
        out_specs=[pl.BlockSpec((tm, D_MODEL), row), pl.BlockSpec((tm, D_MODEL), row)],
        out_shape=[jax.ShapeDtypeStruct((m, D_MODEL), F32), jax.ShapeDtypeStruct((m, D_MODEL), BF16)],
        compiler_params=pltpu.CompilerParams(
            dimension_semantics=("arbitrary",), vmem_limit_bytes=VMEM_LIMIT),
        name="outproj",
    )(x, att, cz, w_out, g2)


def _ffn_kernel(x1_ref, h2_ref, wu_ref, wd_ref, *rest, groups_per_row):
    y_ref = rest[-2] if groups_per_row else rest[-1]

    @pl.when(pl.program_id(1) == 0)
    def _():
        y_ref[...] = x1_ref[...]

    def ffn_rows(rows):
        a = _dot(h2_ref[rows, :], wu_ref[...])
        g = jnp.square(jnp.maximum(a, 0.0)).astype(BF16)
        y_ref[rows, :] += _dot(g, wd_ref[...])

    if not groups_per_row:
        ffn_rows(slice(None))
        return

    q_ref, kn_ref, vn_ref, kt_ref, vt_ref, bias_ref, biasn_ref, _, o_ref = rest
    step = pl.program_id(0) * pl.num_programs(1) + pl.program_id(1)
    head_base = (step % groups_per_row) * DEC_HEAD_GROUP
    half = h2_ref.shape[0] // 2
    scores = _decode_scores(head_base, q_ref, kn_ref, kt_ref, bias_ref, biasn_ref)
    ffn_rows(slice(0, half))
    _decode_outputs(scores, vn_ref, vt_ref, o_ref)
    ffn_rows(slice(half, 2 * half))


def _ffn(x1, h2, w_up, w_down, decode=None, batch0=0, nbatch=0, *, tm=512, tf=1024):
    m = x1.shape[0]
    assert m % tm == 0 and D_FF % tf == 0
    nf = D_FF // tf
    row = lambda i, f: (i, 0)
    once = dict(pipeline_mode=pl.Buffered(1)) if decode is not None else {}
    in_specs = [
        pl.BlockSpec((tm, D_MODEL), row, **once),
        pl.BlockSpec((tm, D_MODEL), row, **once),
        pl.BlockSpec((D_MODEL, tf), lambda i, f: (0, f)),
        pl.BlockSpec((tf, D_MODEL), lambda i, f: (f, 0)),
    ]
    y_spec = pl.BlockSpec((tm, D_MODEL), row)
    y_shape = jax.ShapeDtypeStruct((m, D_MODEL), F32)
    if decode is None:
        return pl.pallas_call(
            functools.partial(_ffn_kernel, groups_per_row=0),
            grid=(m // tm, nf), in_specs=in_specs, out_specs=y_spec, out_shape=y_shape,
            compiler_params=pltpu.CompilerParams(
                dimension_semantics=("arbitrary", "arbitrary"), vmem_limit_bytes=VMEM_LIMIT),
            name="ffn",
        )(x1, h2, w_up, w_down)

    d_specs, d_operands, d_out_spec, d_out_shape = _decode_host_specs(
        decode, batch0, nbatch, (m // tm) * nf, lambda i, f: i * nf + f)
    return pl.pallas_call(
        functools.partial(_ffn_kernel, groups_per_row=N_HEADS // DEC_HEAD_GROUP),
        grid=(m // tm, nf),
        in_specs=in_specs + d_specs,
        out_specs=[y_spec, d_out_spec],
        out_shape=[y_shape, d_out_shape],
        compiler_params=pltpu.CompilerParams(
            dimension_semantics=("arbitrary", "arbitrary"), vmem_limit_bytes=VMEM_LIMIT_HOST),
        name="ffn_with_decode_attention",
    )(x1, h2, w_up, w_down, *d_operands)


def _head_sum_matrices():
    head_of_col = np.arange(ATT_WIDTH) // HEAD_DIM
    red = (head_of_col[:, None] == np.arange(LANES)[None, :]).astype(np.float32)
    return jnp.asarray(red, BF16), jnp.asarray(red.T, BF16)


def kernel(x_prompt, x_sample, state_k, state_v, state_conv, norm1_g, w_in, q_norm_g, k_norm_g,
           conv_w, w_out, norm2_g, w_up, w_down):
    depth = w_in.shape[0]
    assert depth == 1
    bp, sp, _ = x_prompt.shape
    nb, t, _ = x_sample.shape
    past = state_k.shape[2]
    assert bp == 1 and t == 8 and past == MAX_WINDOW and sp >= MAX_WINDOW

    red, expand = _head_sum_matrices()
    g1 = norm1_g[0][None]
    g2 = norm2_g[0][None]
    qg = jnp.tile(q_norm_g[0], N_HEADS)[None]
    kg = jnp.tile(k_norm_g[0], N_HEADS)[None]
    cw = conv_w[0]
    w_in_b = w_in[0].astype(BF16)

    xp = x_prompt[0]
    zero_conv = jnp.zeros((CONV_K - 1, CONV_WIDTH), F32)
    qp, kp, vp, czp, up, w_out_b, w_up_b, w_down_b = _inproj(
        xp, zero_conv, g1, w_in_b, qg, kg, red, expand, cw, grouped=False,
        casts=(w_out[0], w_up[0], w_down[0]))

    xs = x_sample.reshape(nb * t, D_MODEL)
    qs, ks, vs, czs, us = _inproj(xs, state_conv[0], g1, w_in_b, qg, kg, red, expand, cw, grouped=True)
    kt = jnp.transpose(state_k[0], (0, 2, 3, 1))
    vt = jnp.transpose(state_v[0], (0, 2, 3, 1))
    per_batch = lambda a: a.reshape(nb, t, ATT_WIDTH)
    decode = (per_batch(qs), per_batch(ks), per_batch(vs), kt, vt)
    gpr = N_HEADS // DEC_HEAD_GROUP
    nb_attn = (sp // SUPER) * (ATT_WIDTH // LANES) * PROMPT_QUARTERS // gpr
    assert 0 < nb_attn < nb

    attp, atts_a = _prompt_attention(qp, kp, vp, decode, 0, nb_attn)
    x1p, h2p = _outproj(xp, attp, czp, w_out_b, g2)
    ffn_tm = 1024
    ffn_tf = D_FF * (sp // ffn_tm) // ((nb - nb_attn) * gpr)
    yp, atts_f = _ffn(x1p, h2p, w_up_b, w_down_b, decode, nb_attn, nb - nb_attn, tm=ffn_tm, tf=ffn_tf)
    atts = jnp.concatenate([atts_a, atts_f], axis=0).reshape(nb * t, ATT_WIDTH)
    x1s, h2s = _outproj(xs, atts, czs, w_out_b, g2)
    ys = _ffn(x1s, h2s, w_up_b, w_down_b)

    keep = min(MAX_WINDOW, sp)
    new_k_prompt = kp[sp - keep:].reshape(1, 1, keep, N_HEADS, HEAD_DIM)
    new_v_prompt = vp[sp - keep:].reshape(1, 1, keep, N_HEADS, HEAD_DIM)
    new_conv_prompt = up[8 - (CONV_K - 1):].reshape(1, 1, CONV_K - 1, CONV_WIDTH)
    new_k_sample = ks.reshape(1, nb, t, N_HEADS, HEAD_DIM)
    new_v_sample = vs.reshape(1, nb, t, N_HEADS, HEAD_DIM)
    new_conv_sample = us.reshape(nb, t, CONV_WIDTH)[:, t - (CONV_K - 1):][None]
    return (yp[None], ys.reshape(nb, t, D_MODEL), new_k_prompt, new_v_prompt, new_conv_prompt,
            new_k_sample, new_v_sample, new_conv_sample)
```

```python
import functools

import numpy as np
import jax
import jax.numpy as jnp
from jax import lax
from jax.experimental import pallas as pl
from jax.experimental.pallas import tpu as pltpu

F32 = jnp.float32
BF16 = jnp.bfloat16

D_MODEL = 2048
ATT_WIDTH = 1024
CONV_WIDTH = 1024
HEAD_DIM = 64
N_HEADS = 16
WINDOWS = (128, 512, 2048)
DILATIONS = (1, 4, 16)
MAX_WINDOW = 2048
CONV_K = 3
D_FF = 4 * D_MODEL
EPS = 1e-6
IN_COLS = 6 * 1024

LANES = 128
Q_TILE = 128
SUPER = MAX_WINDOW
VMEM_LIMIT = 56 * 1024 * 1024
VMEM_LIMIT_HOST = 60 * 1024 * 1024


def _dot(a, b):
    return jnp.dot(a, b, preferred_element_type=F32)


def _dot_nt(a, b):
    return lax.dot_general(a, b, (((1,), (1,)), ((), ())), preferred_element_type=F32)


def _alibi_slopes():
    return 2.0 ** (-8.0 * np.arange(1, N_HEADS + 1, dtype=np.float64) / N_HEADS)


def _head_rmsnorm(p, gain_row, red, expand):
    ss = _dot((p * p).astype(BF16), red)
    inv = lax.rsqrt(ss * (1.0 / HEAD_DIM) + EPS)
    hi = inv.astype(BF16)
    lo = (inv - hi.astype(F32)).astype(BF16)
    inv_e = _dot(hi, expand) + _dot(lo, expand)
    return p * inv_e * gain_row


def _inproj_kernel(*refs, tm, grouped, n_cast):
    x_ref, g1_ref, w_ref, qg_ref, kg_ref, red_ref, exp_ref, cw_ref, past_ref = refs[:9]
    cast_in = refs[9:9 + n_cast]
    q_ref, k_ref, v_ref, cz_ref, u_ref = refs[9 + n_cast:14 + n_cast]
    cast_out = refs[14 + n_cast:14 + 2 * n_cast]
    h_scr, gb_scr, c_scr, ubuf = refs[14 + 2 * n_cast:]
    i = pl.program_id(0)
    j = pl.program_id(1)

    @pl.when(j == 0)
    def _():
        x = x_ref[...]
        ms = jnp.mean(x * x, axis=-1, keepdims=True)
        h_scr[...] = (x * lax.rsqrt(ms + EPS) * g1_ref[...]).astype(BF16)

    proj = _dot(h_scr[...], w_ref[...])

    for src, dst in zip(cast_in, cast_out):
        dst[...] = src[...].astype(BF16)

    @pl.when(j == 0)
    def _():
        q_ref[...] = _head_rmsnorm(proj, qg_ref[...], red_ref[...], exp_ref[...]) * (HEAD_DIM ** -0.5)

    @pl.when(j == 1)
    def _():
        k_ref[...] = _head_rmsnorm(proj, kg_ref[...], red_ref[...], exp_ref[...])

    @pl.when(j == 2)
    def _():
        v_ref[...] = proj

    @pl.when(j == 3)
    def _():
        gb_scr[...] = proj

    @pl.when(j == 4)
    def _():
        c_scr[...] = proj

    @pl.when(j == 5)
    def _():
        u = c_scr[...] * proj
        cw = cw_ref[...]
        if grouped:
            g = tm // 8
            u3 = u.reshape(g, 8, CONV_WIDTH)
            past = past_ref[...]
            p0 = past[:, 0:1, :]
            p1 = past[:, 1:2, :]
            tok = lax.broadcasted_iota(jnp.int32, u3.shape, 1)
            prev1 = jnp.where(tok == 0, p1, pltpu.roll(u3, 1, axis=1))
            prev2 = jnp.where(tok == 0, p0, jnp.where(tok == 1, p1, pltpu.roll(u3, 2, axis=1)))
            conv = cw[0:1][None] * prev2 + cw[1:2][None] * prev1 + cw[2:3][None] * u3
            cz_ref[...] = (gb_scr[...] * conv.reshape(tm, CONV_WIDTH)).astype(BF16)
            u_ref[...] = u
        else:
            @pl.when(i == 0)
            def _():
                ubuf[0:8, :] = jnp.zeros((8, CONV_WIDTH), F32)
                ubuf[6:8, :] = past_ref[...]

            ubuf[8:tm + 8, :] = u
            conv = cw[0:1] * ubuf[6:tm + 6, :] + cw[1:2] * ubuf[7:tm + 7, :] + cw[2:3] * u
            cz_ref[...] = (gb_scr[...] * conv).astype(BF16)
            tail = ubuf[tm:tm + 8, :]
            ubuf[0:8, :] = tail
            u_ref[...] = tail


CAST_SLABS = 64


def _inproj(x, past, g1, w_in, qg, kg, red, expand, cw, *, grouped, casts=(), tm=512):
    m = x.shape[0]
    assert m % tm == 0
    nt = m // tm
    row = lambda i, j: (i, 0)
    const2 = lambda i, j: (0, 0)
    assert not casts or nt * 6 >= CAST_SLABS
    slab = lambda i, j: (jnp.minimum(i * 6 + j, CAST_SLABS - 1), 0)
    cast_specs = [pl.BlockSpec((a.shape[0] // CAST_SLABS, a.shape[1]), slab) for a in casts]
    cast_shapes = [jax.ShapeDtypeStruct(a.shape, BF16) for a in casts]
    assert all(a.shape[0] % (16 * CAST_SLABS) == 0 for a in casts)
    if grouped:
        past_spec = pl.BlockSpec((tm // 8, CONV_K - 1, CONV_WIDTH), lambda i, j: (i, 0, 0))
        u_shape = jax.ShapeDtypeStruct((m, CONV_WIDTH), F32)
        u_spec = pl.BlockSpec((tm, CONV_WIDTH), row)
    else:
        past_spec = pl.BlockSpec((CONV_K - 1, CONV_WIDTH), const2)
        u_shape = jax.ShapeDtypeStruct((8, CONV_WIDTH), F32)
        u_spec = pl.BlockSpec((8, CONV_WIDTH), const2)
    out_f32 = jax.ShapeDtypeStruct((m, ATT_WIDTH), F32)
    return pl.pallas_call(
        functools.partial(_inproj_kernel, tm=tm, grouped=grouped, n_cast=len(casts)),
        grid=(nt, 6),
        in_specs=[
            pl.BlockSpec((tm, D_MODEL), row),
            pl.BlockSpec((1, D_MODEL), const2),
            pl.BlockSpec((D_MODEL, 1024), lambda i, j: (0, j)),
            pl.BlockSpec((1, ATT_WIDTH), const2),
            pl.BlockSpec((1, ATT_WIDTH), const2),
            pl.BlockSpec((ATT_WIDTH, LANES), const2),
            pl.BlockSpec((LANES, ATT_WIDTH), const2),
            pl.BlockSpec((CONV_K, CONV_WIDTH), const2),
            past_spec,
        ] + cast_specs,
        out_specs=[
            pl.BlockSpec((tm, ATT_WIDTH), row),
            pl.BlockSpec((tm, ATT_WIDTH), row),
            pl.BlockSpec((tm, ATT_WIDTH), row),
            pl.BlockSpec((tm, CONV_WIDTH), row),
            u_spec,
        ] + cast_specs,
        out_shape=[out_f32, out_f32, out_f32,
                   jax.ShapeDtypeStruct((m, CONV_WIDTH), BF16), u_shape] + cast_shapes,
        scratch_shapes=[
            pltpu.VMEM((tm, D_MODEL), BF16),
            pltpu.VMEM((tm, CONV_WIDTH), F32),
            pltpu.VMEM((tm, CONV_WIDTH), F32),
            pltpu.VMEM((tm + 8, CONV_WIDTH), F32),
        ],
        compiler_params=pltpu.CompilerParams(
            dimension_semantics=("arbitrary", "arbitrary"), vmem_limit_bytes=VMEM_LIMIT),
        name="inproj_grouped" if grouped else "inproj_seq",
    )(x, g1, w_in, qg, kg, red, expand, cw, past, *casts)


def _prompt_bias_table():
    qi = np.arange(Q_TILE)[:, None]
    kc = np.arange(2 * Q_TILE)[None, :]
    jdist = (Q_TILE + qi - kc).astype(np.float64)
    valid = (jdist >= 0) & (jdist <= Q_TILE)
    slopes = _alibi_slopes()
    tab = np.empty((2, len(DILATIONS), N_HEADS, Q_TILE, 2 * Q_TILE), np.float32)
    for f in range(2):
        ok = valid & (kc >= Q_TILE) if f else valid
        for p, d in enumerate(DILATIONS):
            for h in range(N_HEADS):
                tab[f, p, h] = np.where(ok, -slopes[h] * d * jdist, -np.inf)
    return tab


def _prompt_attn_kernel(q_ref, kc_ref, kp_ref, vc_ref, vp_ref, bias_ref,
                        dq_ref, dkn_ref, dvn_ref, dkt_ref, dvt_ref, dbias_ref, dbiasn_ref,
                        o_ref, do_ref, ktail_ref, vtail_ref, o_scr, m_scr, l_scr):
    first = jnp.where(pl.program_id(0) == 0, 1, 0)
    quarter = pl.program_id(2)
    lane = lax.broadcasted_iota(jnp.int32, (Q_TILE, LANES), 1)
    lo = lane < HEAD_DIM
    hi = jnp.logical_not(lo)

    def attend(units):
        def scores(unit):
            pat, variant, rows, kprev_ref, vprev_ref, prev = unit
            q = q_ref[rows, :]
            k2 = jnp.concatenate([kprev_ref[prev, :], kc_ref[rows, :]], axis=0).astype(BF16)
            return [_dot_nt(jnp.where(sel, q, 0.0).astype(BF16), k2) + bias_ref[variant, pat, e]
                    for e, sel in enumerate((lo, hi))]

        def softmax(ss):
            out = []
            for s in ss:
                m = jnp.max(s, axis=1, keepdims=True)
                p = jnp.exp(s - m)
                out.append((m, jnp.sum(p, axis=1, keepdims=True), p.astype(BF16)))
            return out

        def output(unit, stats):
            pat, _, rows, _, vprev_ref, prev = unit
            v2 = jnp.concatenate([vprev_ref[prev, :], vc_ref[rows, :]], axis=0).astype(BF16)
            (m0, l0, p0), (m1, l1, p1) = stats
            o_scr[pat, rows, :] = jnp.where(lo, _dot(p0, v2), _dot(p1, v2))
            m_scr[pat, rows, :] = jnp.where(lo, m0, m1)
            l_scr[pat, rows, :] = jnp.where(lo, l0, l1)

        all_scores = [scores(unit) for unit in units]
        all_stats = [softmax(ss) for ss in all_scores]
        for unit, stats in zip(units, all_stats):
            output(unit, stats)

    def unit16(r):
        rows = pl.ds(r, Q_TILE, stride=16)
        return (2, first, rows, kp_ref, vp_ref, rows)

    def unit4(start, from_prev_block):
        rows = pl.ds(start, Q_TILE, stride=4)
        if from_prev_block:
            return (1, first, rows, kp_ref, vp_ref, pl.ds(SUPER - 512 + start, Q_TILE, stride=4))
        return (1, 0, rows, kc_ref, vc_ref, pl.ds(start - 512, Q_TILE, stride=4))

    def unit1(start, from_prev_block):
        rows = pl.ds(start, Q_TILE)
        if from_prev_block:
            return (0, first, rows, kp_ref, vp_ref, pl.ds(SUPER - Q_TILE, Q_TILE))
        return (0, 0, rows, kc_ref, vc_ref, pl.ds(start - Q_TILE, Q_TILE))

    def pat16(g):
        return [unit16(g * 4 + u) for u in range(4)]

    first4 = [unit4(r4, True) for r4 in range(4)]
    rest4 = [unit4(512 * c + r4, False) for c in range(1, SUPER // 512) for r4 in range(4)]

    def pat1(g):
        return [unit1((1 + 5 * g + u) * Q_TILE, False) for u in range(5)]

    def merge(c, carry):
        rows = pl.ds(pl.multiple_of(c * 256, 256), 256)
        ms = [m_scr[p, rows, :] for p in range(3)]
        mx = jnp.maximum(jnp.maximum(ms[0], ms[1]), ms[2])
        num = jnp.zeros((256, LANES), F32)
        den = jnp.zeros((256, LANES), F32)
        for p in range(3):
            w = jnp.exp(ms[p] - mx)
            num = num + w * o_scr[p, rows, :]
            den = den + w * l_scr[p, rows, :]
        o_ref[rows, :] = (num / den).astype(BF16)
        return carry

    step = (pl.program_id(0) * pl.num_programs(1) + pl.program_id(1)) * pl.num_programs(2) + quarter
    head_base = (step % (N_HEADS // DEC_HEAD_GROUP)) * DEC_HEAD_GROUP

    def quarter_body(groups):
        dec = _DecodeGroup(head_base, dq_ref, dkn_ref, dvn_ref, dkt_ref, dvt_ref, dbias_ref, dbiasn_ref,
                           do_ref)
        dec.scores()
        attend(groups[0])
        dec.outputs()
        for g in groups[1:]:
            attend(g)

    @pl.when(quarter == 0)
    def _():
        ktail_ref[...] = kc_ref[...]
        vtail_ref[...] = vc_ref[...]
        quarter_body([pat16(0), pat16(1), pat1(0)])

    @pl.when(quarter == 1)
    def _():
        quarter_body([pat16(2), pat16(3), pat1(1)])

    @pl.when(quarter == 2)
    def _():
        quarter_body([first4 + [unit1(0, True)], rest4[:6], rest4[6:]])

    @pl.when(quarter == 3)
    def _():
        quarter_body([pat1(2)])
        lax.fori_loop(0, SUPER // 256, merge, 0)


PROMPT_QUARTERS = 4


def _prompt_attention(q, k, v, decode, batch0, nbatch):
    s = q.shape[0]
    assert s % SUPER == 0
    bias = jnp.asarray(_prompt_bias_table())
    cur = lambda b, p, c: (b, p)
    prev = lambda b, p, c: (jnp.maximum(b - 1, 0), p)
    blk = (SUPER, LANES)
    npair = ATT_WIDTH // LANES
    grid = (s // SUPER, npair, PROMPT_QUARTERS)
    step_of = lambda b, p, c: (b * npair + p) * PROMPT_QUARTERS + c
    d_specs, d_operands, d_out_spec, d_out_shape = _decode_host_specs(
        decode, batch0, nbatch, grid[0] * grid[1] * grid[2], step_of)
    tail_spec = pl.BlockSpec(blk, lambda b, p, c: (0, p))
    tail_shape = jax.ShapeDtypeStruct((SUPER, ATT_WIDTH), F32)
    return pl.pallas_call(
        _prompt_attn_kernel,
        grid=grid,
        in_specs=[
            pl.BlockSpec(blk, cur),
            pl.BlockSpec(blk, cur),
            pl.BlockSpec(blk, prev),
            pl.BlockSpec(blk, cur),
            pl.BlockSpec(blk, prev),
            pl.BlockSpec((2, 3, 2, Q_TILE, 2 * Q_TILE), lambda b, p, c: (0, 0, p, 0, 0)),
        ] + d_specs,
        out_specs=[pl.BlockSpec(blk, cur), d_out_spec, tail_spec, tail_spec],
        out_shape=[jax.ShapeDtypeStruct((s, ATT_WIDTH), BF16), d_out_shape, tail_shape, tail_shape],
        scratch_shapes=[pltpu.VMEM((3, SUPER, LANES), F32)] * 3,
        compiler_params=pltpu.CompilerParams(
            dimension_semantics=("arbitrary", "arbitrary", "arbitrary"), vmem_limit_bytes=VMEM_LIMIT),
        name="prompt_attention_with_decode_attention",
    )(q, k, k, v, v, bias, *d_operands)


DEC_SEQ = 8
DEC_HEAD_GROUP = 8


def _decode_bias_tables(past_len):
    dist_max = MAX_WINDOW
    mult = np.zeros(dist_max + 1, np.float64)
    for w, d in zip(WINDOWS, DILATIONS):
        mult[np.arange(w // d + 1) * d] += 1.0
    slopes = _alibi_slopes()

    def table(dist):
        ok = (dist >= 0) & (dist <= dist_max)
        dc = np.clip(dist, 0, dist_max)
        mu = np.where(ok, mult[dc], 0.0)
        with np.errstate(divide="ignore"):
            logm = np.where(mu > 0, np.log(np.maximum(mu, 1e-30)), -np.inf)
        return (-slopes[:, None, None] * dc[None].astype(np.float64) + logm[None]).astype(np.float32)

    qpos = past_len + np.arange(DEC_SEQ)
    old = table(qpos[:, None] - np.arange(past_len)[None, :])
    new = np.full((N_HEADS, DEC_SEQ, 2 * DEC_SEQ), -np.inf, np.float32)
    new[:, :, :DEC_SEQ] = table(qpos[:, None] - qpos[None, :])
    return old, new


def _pad16(a):
    return jnp.concatenate([a, jnp.zeros_like(a)], axis=0).astype(BF16)


class _DecodeGroup:
    def __init__(self, head_base, q_ref, kn_ref, vn_ref, kt_ref, vt_ref, bias_ref, biasn_ref, o_ref):
        self.head_base = head_base
        self.refs = (q_ref, kn_ref, vn_ref, kt_ref, vt_ref, bias_ref, biasn_ref, o_ref)

    def scores(self):
        q_ref, kn_ref, _, kt_ref, _, bias_ref, biasn_ref, _ = self.refs
        self.s = []
        for u in range(DEC_HEAD_GROUP):
            c = slice(u * HEAD_DIM, (u + 1) * HEAD_DIM)
            q = _pad16(q_ref[:, c])
            s = _dot(q, kt_ref[u].astype(BF16))[:DEC_SEQ] + bias_ref[self.head_base + u]
            sn = _dot_nt(q, _pad16(kn_ref[:, c]))[:DEC_SEQ] + biasn_ref[self.head_base + u]
            self.s.append((s, sn))

    def outputs(self):
        _, _, vn_ref, _, vt_ref, _, _, o_ref = self.refs
        probs = []
        for s, sn in self.s:
            m = jnp.maximum(jnp.max(s, axis=1, keepdims=True), jnp.max(sn, axis=1, keepdims=True))
            probs.append((jnp.exp(s - m), jnp.exp(sn - m)))
        for u, (p, pn) in enumerate(probs):
            c = slice(u * HEAD_DIM, (u + 1) * HEAD_DIM)
            l = jnp.sum(p, axis=1, keepdims=True) + jnp.sum(pn, axis=1, keepdims=True)
            o = (_dot_nt(_pad16(p), vt_ref[u].astype(BF16))[:DEC_SEQ]
                 + _dot(_pad16(pn), _pad16(vn_ref[:, c]))[:DEC_SEQ])
            o_ref[:, c] = o / l


def _decode_host_specs(decode, batch0, nbatch, nsteps, step_of):
    q, kn, vn, kt, vt = decode
    nb, _, _, past = kt.shape
    gpr = N_HEADS // DEC_HEAD_GROUP
    gw = DEC_HEAD_GROUP * HEAD_DIM
    assert q.shape == (nb, DEC_SEQ, ATT_WIDTH) and nsteps == nbatch * gpr and batch0 + nbatch <= nb
    old, new = _decode_bias_tables(past)
    small = pl.BlockSpec((None, DEC_SEQ, gw),
                         lambda *g: (batch0 + step_of(*g) // gpr, 0, step_of(*g) % gpr))
    big = pl.BlockSpec((None, DEC_HEAD_GROUP, HEAD_DIM, past),
                       lambda *g: (batch0 + step_of(*g) // gpr, step_of(*g) % gpr, 0, 0))
    const3 = lambda *g: (0, 0, 0)
    in_specs = [small, small, small, big, big,
                pl.BlockSpec((N_HEADS, DEC_SEQ, past), const3),
                pl.BlockSpec((N_HEADS, DEC_SEQ, 2 * DEC_SEQ), const3)]
    out_spec = pl.BlockSpec((None, DEC_SEQ, gw), lambda *g: (step_of(*g) // gpr, 0, step_of(*g) % gpr))
    out_shape = jax.ShapeDtypeStruct((nbatch, DEC_SEQ, ATT_WIDTH), F32)
    return in_specs, (q, kn, vn, kt, vt, jnp.asarray(old), jnp.asarray(new)), out_spec, out_shape


def _outproj_kernel(x_ref, att_ref, cz_ref, wo_ref, g2_ref, x1_ref, h2_ref):
    y = (x_ref[...]
         + _dot(att_ref[...].astype(BF16), wo_ref[0:ATT_WIDTH, :])
         + _dot(cz_ref[...], wo_ref[ATT_WIDTH:D_MODEL, :]))
    x1_ref[...] = y
    ms = jnp.mean(y * y, axis=-1, keepdims=True)
    h2_ref[...] = (y * lax.rsqrt(ms + EPS) * g2_ref[...]).astype(BF16)


def _outproj(x, att, cz, w_out, g2, *, tm=512):
    m = x.shape[0]
    assert m % tm == 0
    row = lambda i: (i, 0)
    const = lambda i: (0, 0)
    return pl.pallas_call(
        _outproj_kernel,
        grid=(m // tm,),
        in_specs=[
            pl.BlockSpec((tm, D_MODEL), row),
            pl.BlockSpec((tm, ATT_WIDTH), row),
            pl.BlockSpec((tm, CONV_WIDTH), row),
            pl.BlockSpec((D_MODEL, D_MODEL), const),
            pl.BlockSpec((1, D_MODEL), const),
        ],
        out_specs=[pl.BlockSpec((tm, D_MODEL), row), pl.BlockSpec((tm, D_MODEL), row)],
        out_shape=[jax.ShapeDtypeStruct((m, D_MODEL), F32), jax.ShapeDtypeStruct((m, D_MODEL), BF16)],
        compiler_params=pltpu.CompilerParams(
            dimension_semantics=("arbitrary",), vmem_limit_bytes=VMEM_LIMIT),
        name="outproj",
    )(x, att, cz, w_out, g2)


def _ffn_kernel(x1_ref, h2_ref, wu_ref, wd_ref, *rest, groups_per_row):
    y_ref = rest[-2] if groups_per_row else rest[-1]

    @pl.when(pl.program_id(1) == 0)
    def _():
        y_ref[...] = x1_ref[...]

    def ffn_rows(rows):
        a = _dot(h2_ref[rows, :], wu_ref[...])
        g = jnp.square(jnp.maximum(a, 0.0)).astype(BF16)
        y_ref[rows, :] += _dot(g, wd_ref[...])

    if not groups_per_row:
        ffn_rows(slice(None))
        return

    q_ref, kn_ref, vn_ref, kt_ref, vt_ref, bias_ref, biasn_ref, _, o_ref = rest
    step = pl.program_id(0) * pl.num_programs(1) + pl.program_id(1)
    head_base = (step % groups_per_row) * DEC_HEAD_GROUP
    dec = _DecodeGroup(head_base, q_ref, kn_ref, vn_ref, kt_ref, vt_ref, bias_ref, biasn_ref, o_ref)
    half = h2_ref.shape[0] // 2
    dec.scores()
    ffn_rows(slice(0, half))
    dec.outputs()
    ffn_rows(slice(half, 2 * half))


def _ffn(x1, h2, w_up, w_down, decode=None, batch0=0, nbatch=0, *, tm=512, tf=1024):
    m = x1.shape[0]
    assert m % tm == 0 and D_FF % tf == 0
    nf = D_FF // tf
    row = lambda i, f: (i, 0)
    once = dict(pipeline_mode=pl.Buffered(1)) if decode is not None else {}
    in_specs = [
        pl.BlockSpec((tm, D_MODEL), row, **once),
        pl.BlockSpec((tm, D_MODEL), row),
        pl.BlockSpec((D_MODEL, tf), lambda i, f: (0, f)),
        pl.BlockSpec((tf, D_MODEL), lambda i, f: (f, 0)),
    ]
    y_spec = pl.BlockSpec((tm, D_MODEL), row)
    y_shape = jax.ShapeDtypeStruct((m, D_MODEL), F32)
    if decode is None:
        return pl.pallas_call(
            functools.partial(_ffn_kernel, groups_per_row=0),
            grid=(m // tm, nf), in_specs=in_specs, out_specs=y_spec, out_shape=y_shape,
            compiler_params=pltpu.CompilerParams(
                dimension_semantics=("arbitrary", "arbitrary"), vmem_limit_bytes=VMEM_LIMIT),
            name="ffn",
        )(x1, h2, w_up, w_down)

    d_specs, d_operands, d_out_spec, d_out_shape = _decode_host_specs(
        decode, batch0, nbatch, (m // tm) * nf, lambda i, f: i * nf + f)
    return pl.pallas_call(
        functools.partial(_ffn_kernel, groups_per_row=N_HEADS // DEC_HEAD_GROUP),
        grid=(m // tm, nf),
        in_specs=in_specs + d_specs,
        out_specs=[y_spec, d_out_spec],
        out_shape=[y_shape, d_out_shape],
        compiler_params=pltpu.CompilerParams(
            dimension_semantics=("arbitrary", "arbitrary"), vmem_limit_bytes=VMEM_LIMIT_HOST),
        name="ffn_with_decode_attention",
    )(x1, h2, w_up, w_down, *d_operands)


def _head_sum_matrices():
    head_of_col = np.arange(ATT_WIDTH) // HEAD_DIM
    red = (head_of_col[:, None] == np.arange(LANES)[None, :]).astype(np.float32)
    return jnp.asarray(red, BF16), jnp.asarray(red.T, BF16)


def kernel(x_prompt, x_sample, state_k, state_v, state_conv, norm1_g, w_in, q_norm_g, k_norm_g,
           conv_w, w_out, norm2_g, w_up, w_down):
    depth = w_in.shape[0]
    assert depth == 1
    bp, sp, _ = x_prompt.shape
    nb, t, _ = x_sample.shape
    past = state_k.shape[2]
    assert bp == 1 and t == 8 and past == MAX_WINDOW and sp >= MAX_WINDOW

    red, expand = _head_sum_matrices()
    g1 = norm1_g[0][None]
    g2 = norm2_g[0][None]
    qg = jnp.tile(q_norm_g[0], N_HEADS)[None]
    kg = jnp.tile(k_norm_g[0], N_HEADS)[None]
    cw = conv_w[0]
    w_in_b = w_in[0].astype(BF16)

    xp = x_prompt[0]
    zero_conv = jnp.zeros((CONV_K - 1, CONV_WIDTH), F32)
    qp, kp, vp, czp, up, w_out_b, w_up_b, w_down_b = _inproj(
        xp, zero_conv, g1, w_in_b, qg, kg, red, expand, cw, grouped=False,
        casts=(w_out[0], w_up[0], w_down[0]))

    xs = x_sample.reshape(nb * t, D_MODEL)
    qs, ks, vs, czs, us = _inproj(xs, state_conv[0], g1, w_in_b, qg, kg, red, expand, cw, grouped=True)
    kt = jnp.transpose(state_k[0], (0, 2, 3, 1))
    vt = jnp.transpose(state_v[0], (0, 2, 3, 1))
    per_batch = lambda a: a.reshape(nb, t, ATT_WIDTH)
    decode = (per_batch(qs), per_batch(ks), per_batch(vs), kt, vt)
    gpr = N_HEADS // DEC_HEAD_GROUP
    nb_attn = (sp // SUPER) * (ATT_WIDTH // LANES) * PROMPT_QUARTERS // gpr
    assert 0 < nb_attn < nb

    attp, atts_a, k_tail, v_tail = _prompt_attention(qp, kp, vp, decode, 0, nb_attn)
    x1p, h2p = _outproj(xp, attp, czp, w_out_b, g2)
    ffn_tm = 1024
    ffn_tf = D_FF * (sp // ffn_tm) // ((nb - nb_attn) * gpr)
    yp, atts_f = _ffn(x1p, h2p, w_up_b, w_down_b, decode, nb_attn, nb - nb_attn, tm=ffn_tm, tf=ffn_tf)
    atts = jnp.concatenate([atts_a, atts_f], axis=0).reshape(nb * t, ATT_WIDTH)
    x1s, h2s = _outproj(xs, atts, czs, w_out_b, g2)
    ys = _ffn(x1s, h2s, w_up_b, w_down_b)

    keep = min(MAX_WINDOW, sp)
    assert keep == SUPER
    new_k_prompt = k_tail.reshape(1, 1, keep, N_HEADS, HEAD_DIM)
    new_v_prompt = v_tail.reshape(1, 1, keep, N_HEADS, HEAD_DIM)
    new_conv_prompt = up[8 - (CONV_K - 1):].reshape(1, 1, CONV_K - 1, CONV_WIDTH)
    new_k_sample = ks.reshape(1, nb, t, N_HEADS, HEAD_DIM)
    new_v_sample = vs.reshape(1, nb, t, N_HEADS, HEAD_DIM)
    new_conv_sample = us.reshape(nb, t, CONV_WIDTH)[:, t - (CONV_K - 1):][None]
    return (yp[None], ys.reshape(nb, t, D_MODEL), new_k_prompt, new_v_prompt, new_conv_prompt,
            new_k_sample, new_v_sample, new_conv_sample)
```

```python
import functools

import numpy as np
import jax
import jax.numpy as jnp
from jax import lax
from jax.experimental import pallas as pl
from jax.experimental.pallas import tpu as pltpu

F32 = jnp.float32
BF16 = jnp.bfloat16

D_MODEL = 2048
ATT_WIDTH = 1024
CONV_WIDTH = 1024
HEAD_DIM = 64
N_HEADS = 16
WINDOWS = (128, 512, 2048)
DILATIONS = (1, 4, 16)
MAX_WINDOW = 2048
CONV_K = 3
D_FF = 4 * D_MODEL
EPS = 1e-6

LANES = 128
Q_TILE = 128
SUPER = MAX_WINDOW
VMEM_LIMIT = 56 * 1024 * 1024
VMEM_LIMIT_HOST = 60 * 1024 * 1024

INPROJ_ROWS = 512
OUTPROJ_ROWS = 512
FFN_ROWS, FFN_COLS = 1024, 512


def _dot(a, b):
    return jnp.dot(a, b, preferred_element_type=F32)


def _dot_nt(a, b):
    return lax.dot_general(a, b, (((1,), (1,)), ((), ())), preferred_element_type=F32)


def _alibi_slopes():
    return 2.0 ** (-8.0 * np.arange(1, N_HEADS + 1, dtype=np.float64) / N_HEADS)


def _head_rmsnorm(p, gain_row, red, expand):
    ss = _dot((p * p).astype(BF16), red)
    inv = lax.rsqrt(ss * (1.0 / HEAD_DIM) + EPS)
    hi = inv.astype(BF16)
    lo = (inv - hi.astype(F32)).astype(BF16)
    inv_e = _dot(hi, expand) + _dot(lo, expand)
    return p * inv_e * gain_row


def _inproj_kernel(*refs, tm, grouped, n_cast):
    x_ref, g1_ref, w_ref, qg_ref, kg_ref, red_ref, exp_ref, cw_ref, past_ref = refs[:9]
    cast_in = refs[9:9 + n_cast]
    q_ref, k_ref, v_ref, cz_ref, u_ref = refs[9 + n_cast:14 + n_cast]
    cast_out = refs[14 + n_cast:14 + 2 * n_cast]
    h_scr, gb_scr, c_scr, ubuf = refs[14 + 2 * n_cast:]
    i = pl.program_id(0)
    j = pl.program_id(1)

    @pl.when(j == 0)
    def _():
        x = x_ref[...]
        ms = jnp.mean(x * x, axis=-1, keepdims=True)
        h_scr[...] = (x * lax.rsqrt(ms + EPS) * g1_ref[...]).astype(BF16)

    proj = _dot(h_scr[...], w_ref[...])

    for src, dst in zip(cast_in, cast_out):
        dst[...] = src[...].astype(BF16)

    @pl.when(j == 0)
    def _():
        q_ref[...] = _head_rmsnorm(proj, qg_ref[...], red_ref[...], exp_ref[...]) * (HEAD_DIM ** -0.5)

    @pl.when(j == 1)
    def _():
        k_ref[...] = _head_rmsnorm(proj, kg_ref[...], red_ref[...], exp_ref[...])

    @pl.when(j == 2)
    def _():
        v_ref[...] = proj

    @pl.when(j == 3)
    def _():
        gb_scr[...] = proj

    @pl.when(j == 4)
    def _():
        c_scr[...] = proj

    @pl.when(j == 5)
    def _():
        u = c_scr[...] * proj
        cw = cw_ref[...]
        if grouped:
            g = tm // 8
            u3 = u.reshape(g, 8, CONV_WIDTH)
            past = past_ref[...]
            p0 = past[:, 0:1, :]
            p1 = past[:, 1:2, :]
            tok = lax.broadcasted_iota(jnp.int32, u3.shape, 1)
            prev1 = jnp.where(tok == 0, p1, pltpu.roll(u3, 1, axis=1))
            prev2 = jnp.where(tok == 0, p0, jnp.where(tok == 1, p1, pltpu.roll(u3, 2, axis=1)))
            conv = cw[0:1][None] * prev2 + cw[1:2][None] * prev1 + cw[2:3][None] * u3
            cz_ref[...] = (gb_scr[...] * conv.reshape(tm, CONV_WIDTH)).astype(BF16)
            u_ref[...] = u
        else:
            @pl.when(i == 0)
            def _():
                ubuf[0:8, :] = jnp.zeros((8, CONV_WIDTH), F32)
                ubuf[6:8, :] = past_ref[...]

            ubuf[8:tm + 8, :] = u
            conv = cw[0:1] * ubuf[6:tm + 6, :] + cw[1:2] * ubuf[7:tm + 7, :] + cw[2:3] * u
            cz_ref[...] = (gb_scr[...] * conv).astype(BF16)
            tail = ubuf[tm:tm + 8, :]
            ubuf[0:8, :] = tail
            u_ref[...] = tail


CAST_SLABS = 64


def _inproj(x, past, g1, w_in, qg, kg, red, expand, cw, *, grouped, casts=(), tm=INPROJ_ROWS):
    m = x.shape[0]
    assert m % tm == 0
    nt = m // tm
    row = lambda i, j: (i, 0)
    const2 = lambda i, j: (0, 0)
    assert not casts or nt * 6 >= CAST_SLABS
    slab = lambda i, j: (jnp.minimum(i * 6 + j, CAST_SLABS - 1), 0)
    cast_specs = [pl.BlockSpec((a.shape[0] // CAST_SLABS, a.shape[1]), slab) for a in casts]
    cast_shapes = [jax.ShapeDtypeStruct(a.shape, BF16) for a in casts]
    assert all(a.shape[0] % (16 * CAST_SLABS) == 0 for a in casts)
    if grouped:
        past_spec = pl.BlockSpec((tm // 8, CONV_K - 1, CONV_WIDTH), lambda i, j: (i, 0, 0))
        u_shape = jax.ShapeDtypeStruct((m, CONV_WIDTH), F32)
        u_spec = pl.BlockSpec((tm, CONV_WIDTH), row)
    else:
        past_spec = pl.BlockSpec((CONV_K - 1, CONV_WIDTH), const2)
        u_shape = jax.ShapeDtypeStruct((8, CONV_WIDTH), F32)
        u_spec = pl.BlockSpec((8, CONV_WIDTH), const2)
    out_f32 = jax.ShapeDtypeStruct((m, ATT_WIDTH), F32)
    return pl.pallas_call(
        functools.partial(_inproj_kernel, tm=tm, grouped=grouped, n_cast=len(casts)),
        grid=(nt, 6),
        in_specs=[
            pl.BlockSpec((tm, D_MODEL), row),
            pl.BlockSpec((1, D_MODEL), const2),
            pl.BlockSpec((D_MODEL, 1024), lambda i, j: (0, j)),
            pl.BlockSpec((1, ATT_WIDTH), const2),
            pl.BlockSpec((1, ATT_WIDTH), const2),
            pl.BlockSpec((ATT_WIDTH, LANES), const2),
            pl.BlockSpec((LANES, ATT_WIDTH), const2),
            pl.BlockSpec((CONV_K, CONV_WIDTH), const2),
            past_spec,
        ] + cast_specs,
        out_specs=[
            pl.BlockSpec((tm, ATT_WIDTH), row),
            pl.BlockSpec((tm, ATT_WIDTH), row),
            pl.BlockSpec((tm, ATT_WIDTH), row),
            pl.BlockSpec((tm, CONV_WIDTH), row),
            u_spec,
        ] + cast_specs,
        out_shape=[out_f32, out_f32, out_f32,
                   jax.ShapeDtypeStruct((m, CONV_WIDTH), BF16), u_shape] + cast_shapes,
        scratch_shapes=[
            pltpu.VMEM((tm, D_MODEL), BF16),
            pltpu.VMEM((tm, CONV_WIDTH), F32),
            pltpu.VMEM((tm, CONV_WIDTH), F32),
            pltpu.VMEM((tm + 8, CONV_WIDTH), F32),
        ],
        compiler_params=pltpu.CompilerParams(
            dimension_semantics=("arbitrary", "arbitrary"), vmem_limit_bytes=VMEM_LIMIT),
        name="inproj_grouped" if grouped else "inproj_seq",
    )(x, g1, w_in, qg, kg, red, expand, cw, past, *casts)


def _prompt_bias_table():
    qi = np.arange(Q_TILE)[:, None]
    kc = np.arange(2 * Q_TILE)[None, :]
    jdist = (Q_TILE + qi - kc).astype(np.float64)
    valid = (jdist >= 0) & (jdist <= Q_TILE)
    slopes = _alibi_slopes()
    tab = np.empty((2, len(DILATIONS), N_HEADS, Q_TILE, 2 * Q_TILE), np.float32)
    for f in range(2):
        ok = valid & (kc >= Q_TILE) if f else valid
        for p, d in enumerate(DILATIONS):
            for h in range(N_HEADS):
                tab[f, p, h] = np.where(ok, -slopes[h] * d * jdist, -np.inf)
    return tab


def _prompt_attn_kernel(q_ref, kc_ref, kp_ref, vc_ref, vp_ref, bias_ref,
                        dq_ref, dkn_ref, dvn_ref, dkt_ref, dvt_ref, dbias_ref, dbiasn_ref,
                        o_ref, do_ref, o_scr, m_scr, l_scr):
    first = jnp.where(pl.program_id(0) == 0, 1, 0)
    quarter = pl.program_id(2)
    lane = lax.broadcasted_iota(jnp.int32, (Q_TILE, LANES), 1)
    lo = lane < HEAD_DIM
    hi = jnp.logical_not(lo)

    def attend(units):
        def scores(unit):
            pat, variant, rows, kprev_ref, vprev_ref, prev = unit
            q = q_ref[rows, :]
            k2 = jnp.concatenate([kprev_ref[prev, :], kc_ref[rows, :]], axis=0).astype(BF16)
            return [_dot_nt(jnp.where(sel, q, 0.0).astype(BF16), k2) + bias_ref[variant, pat, e]
                    for e, sel in enumerate((lo, hi))]

        def softmax(ss):
            out = []
            for s in ss:
                m = jnp.max(s, axis=1, keepdims=True)
                p = jnp.exp(s - m)
                out.append((m, jnp.sum(p, axis=1, keepdims=True), p.astype(BF16)))
            return out

        def output(unit, stats):
            pat, _, rows, _, vprev_ref, prev = unit
            v2 = jnp.concatenate([vprev_ref[prev, :], vc_ref[rows, :]], axis=0).astype(BF16)
            (m0, l0, p0), (m1, l1, p1) = stats
            o_scr[pat, rows, :] = jnp.where(lo, _dot(p0, v2), _dot(p1, v2))
            m_scr[pat, rows, :] = jnp.where(lo, m0, m1)
            l_scr[pat, rows, :] = jnp.where(lo, l0, l1)

        all_scores = [scores(unit) for unit in units]
        all_stats = [softmax(ss) for ss in all_scores]
        for unit, stats in zip(units, all_stats):
            output(unit, stats)

    def unit16(r):
        rows = pl.ds(r, Q_TILE, stride=16)
        return (2, first, rows, kp_ref, vp_ref, rows)

    def unit4(start, from_prev_block):
        rows = pl.ds(start, Q_TILE, stride=4)
        if from_prev_block:
            return (1, first, rows, kp_ref, vp_ref, pl.ds(SUPER - 512 + start, Q_TILE, stride=4))
        return (1, 0, rows, kc_ref, vc_ref, pl.ds(start - 512, Q_TILE, stride=4))

    def unit1(start, from_prev_block):
        rows = pl.ds(start, Q_TILE)
        if from_prev_block:
            return (0, first, rows, kp_ref, vp_ref, pl.ds(SUPER - Q_TILE, Q_TILE))
        return (0, 0, rows, kc_ref, vc_ref, pl.ds(start - Q_TILE, Q_TILE))

    def pat16(g):
        return [unit16(g * 4 + u) for u in range(4)]

    first4 = [unit4(r4, True) for r4 in range(4)]
    rest4 = [unit4(512 * c + r4, False) for c in range(1, SUPER // 512) for r4 in range(4)]

    def pat1(g):
        return [unit1((1 + 5 * g + u) * Q_TILE, False) for u in range(5)]

    def merge(c, carry):
        rows = pl.ds(pl.multiple_of(c * 256, 256), 256)
        ms = [m_scr[p, rows, :] for p in range(3)]
        mx = jnp.maximum(jnp.maximum(ms[0], ms[1]), ms[2])
        num = jnp.zeros((256, LANES), F32)
        den = jnp.zeros((256, LANES), F32)
        for p in range(3):
            w = jnp.exp(ms[p] - mx)
            num = num + w * o_scr[p, rows, :]
            den = den + w * l_scr[p, rows, :]
        o_ref[rows, :] = (num / den).astype(BF16)
        return carry

    step = (pl.program_id(0) * pl.num_programs(1) + pl.program_id(1)) * pl.num_programs(2) + quarter
    head_base = (step % (N_HEADS // DEC_HEAD_GROUP)) * DEC_HEAD_GROUP

    def quarter_body(groups):
        dec = _DecodeGroup(head_base, dq_ref, dkn_ref, dvn_ref, dkt_ref, dvt_ref, dbias_ref, dbiasn_ref,
                           do_ref)
        dec.scores()
        attend(groups[0])
        dec.outputs()
        for g in groups[1:]:
            attend(g)

    @pl.when(quarter == 0)
    def _():
        quarter_body([pat16(0), pat16(1), pat1(0)])

    @pl.when(quarter == 1)
    def _():
        quarter_body([pat16(2), pat16(3), pat1(1)])

    @pl.when(quarter == 2)
    def _():
        quarter_body([first4 + [unit1(0, True)], rest4[:6], rest4[6:]])

    @pl.when(quarter == 3)
    def _():
        quarter_body([pat1(2)])
        lax.fori_loop(0, SUPER // 256, merge, 0)


PROMPT_QUARTERS = 4


def _prompt_attention(q, k, v, decode, batch0, nbatch):
    s = q.shape[0]
    assert s % SUPER == 0
    bias = jnp.asarray(_prompt_bias_table())
    cur = lambda b, p, c: (b, p)
    prev = lambda b, p, c: (jnp.maximum(b - 1, 0), p)
    blk = (SUPER, LANES)
    npair = ATT_WIDTH // LANES
    grid = (s // SUPER, npair, PROMPT_QUARTERS)
    step_of = lambda b, p, c: (b * npair + p) * PROMPT_QUARTERS + c
    d_specs, d_operands, d_out_spec, d_out_shape = _decode_host_specs(
        decode, batch0, nbatch, grid[0] * grid[1] * grid[2], step_of)
    return pl.pallas_call(
        _prompt_attn_kernel,
        grid=grid,
        in_specs=[
            pl.BlockSpec(blk, cur),
            pl.BlockSpec(blk, cur),
            pl.BlockSpec(blk, prev),
            pl.BlockSpec(blk, cur),
            pl.BlockSpec(blk, prev),
            pl.BlockSpec((2, 3, 2, Q_TILE, 2 * Q_TILE), lambda b, p, c: (0, 0, p, 0, 0)),
        ] + d_specs,
        out_specs=[pl.BlockSpec(blk, cur), d_out_spec],
        out_shape=[jax.ShapeDtypeStruct((s, ATT_WIDTH), BF16), d_out_shape],
        scratch_shapes=[pltpu.VMEM((3, SUPER, LANES), F32)] * 3,
        compiler_params=pltpu.CompilerParams(
            dimension_semantics=("arbitrary", "arbitrary", "arbitrary"), vmem_limit_bytes=VMEM_LIMIT),
        name="prompt_attention_with_decode_attention",
    )(q, k, k, v, v, bias, *d_operands)


DEC_SEQ = 8
DEC_HEAD_GROUP = 8


def _decode_bias_tables(past_len):
    dist_max = MAX_WINDOW
    mult = np.zeros(dist_max + 1, np.float64)
    for w, d in zip(WINDOWS, DILATIONS):
        mult[np.arange(w // d + 1) * d] += 1.0
    slopes = _alibi_slopes()

    def table(dist):
        ok = (dist >= 0) & (dist <= dist_max)
        dc = np.clip(dist, 0, dist_max)
        mu = np.where(ok, mult[dc], 0.0)
        with np.errstate(divide="ignore"):
            logm = np.where(mu > 0, np.log(np.maximum(mu, 1e-30)), -np.inf)
        return (-slopes[:, None, None] * dc[None].astype(np.float64) + logm[None]).astype(np.float32)

    qpos = past_len + np.arange(DEC_SEQ)
    old = table(qpos[:, None] - np.arange(past_len)[None, :])
    new = np.full((N_HEADS, DEC_SEQ, 2 * DEC_SEQ), -np.inf, np.float32)
    new[:, :, :DEC_SEQ] = table(qpos[:, None] - qpos[None, :])
    return old, new


def _pad16(a):
    return jnp.concatenate([a, jnp.zeros_like(a)], axis=0).astype(BF16)


class _DecodeGroup:
    def __init__(self, head_base, q_ref, kn_ref, vn_ref, kt_ref, vt_ref, bias_ref, biasn_ref, o_ref):
        self.head_base = head_base
        self.refs = (q_ref, kn_ref, vn_ref, kt_ref, vt_ref, bias_ref, biasn_ref, o_ref)

    def scores(self):
        q_ref, kn_ref, _, kt_ref, _, bias_ref, biasn_ref, _ = self.refs
        self.s = []
        for u in range(DEC_HEAD_GROUP):
            c = slice(u * HEAD_DIM, (u + 1) * HEAD_DIM)
            q = _pad16(q_ref[:, c])
            s = _dot(q, kt_ref[u].astype(BF16))[:DEC_SEQ] + bias_ref[self.head_base + u]
            sn = _dot_nt(q, _pad16(kn_ref[:, c]))[:DEC_SEQ] + biasn_ref[self.head_base + u]
            self.s.append((s, sn))

    def outputs(self):
        _, _, vn_ref, _, vt_ref, _, _, o_ref = self.refs
        probs = []
        for s, sn in self.s:
            m = jnp.maximum(jnp.max(s, axis=1, keepdims=True), jnp.max(sn, axis=1, keepdims=True))
            probs.append((jnp.exp(s - m), jnp.exp(sn - m)))
        for u, (p, pn) in enumerate(probs):
            c = slice(u * HEAD_DIM, (u + 1) * HEAD_DIM)
            l = jnp.sum(p, axis=1, keepdims=True) + jnp.sum(pn, axis=1, keepdims=True)
            o = (_dot_nt(_pad16(p), vt_ref[u].astype(BF16))[:DEC_SEQ]
                 + _dot(_pad16(pn), _pad16(vn_ref[:, c]))[:DEC_SEQ])
            o_ref[:, c] = o / l


def _decode_host_specs(decode, batch0, nbatch, nsteps, step_of):
    q, kn, vn, kt, vt = decode
    nb, _, _, past = kt.shape
    gpr = N_HEADS // DEC_HEAD_GROUP
    gw = DEC_HEAD_GROUP * HEAD_DIM
    assert q.shape == (nb, DEC_SEQ, ATT_WIDTH) and nsteps == nbatch * gpr and batch0 + nbatch <= nb
    old, new = _decode_bias_tables(past)
    small = pl.BlockSpec((None, DEC_SEQ, gw),
                         lambda *g: (batch0 + step_of(*g) // gpr, 0, step_of(*g) % gpr))
    big = pl.BlockSpec((None, DEC_HEAD_GROUP, HEAD_DIM, past),
                       lambda *g: (batch0 + step_of(*g) // gpr, step_of(*g) % gpr, 0, 0))
    const3 = lambda *g: (0, 0, 0)
    in_specs = [small, small, small, big, big,
                pl.BlockSpec((N_HEADS, DEC_SEQ, past), const3),
                pl.BlockSpec((N_HEADS, DEC_SEQ, 2 * DEC_SEQ), const3)]
    out_spec = pl.BlockSpec((None, DEC_SEQ, gw), lambda *g: (step_of(*g) // gpr, 0, step_of(*g) % gpr))
    out_shape = jax.ShapeDtypeStruct((nbatch, DEC_SEQ, ATT_WIDTH), F32)
    return in_specs, (q, kn, vn, kt, vt, jnp.asarray(old), jnp.asarray(new)), out_spec, out_shape


def _outproj_kernel(x_ref, att_ref, cz_ref, wo_ref, g2_ref, x1_ref, h2_ref):
    y = (x_ref[...]
         + _dot(att_ref[...].astype(BF16), wo_ref[0:ATT_WIDTH, :])
         + _dot(cz_ref[...], wo_ref[ATT_WIDTH:D_MODEL, :]))
    x1_ref[...] = y
    ms = jnp.mean(y * y, axis=-1, keepdims=True)
    h2_ref[...] = (y * lax.rsqrt(ms + EPS) * g2_ref[...]).astype(BF16)


def _outproj(x, att, cz, w_out, g2, *, tm=OUTPROJ_ROWS):
    m = x.shape[0]
    assert m % tm == 0
    row = lambda i: (i, 0)
    const = lambda i: (0, 0)
    return pl.pallas_call(
        _outproj_kernel,
        grid=(m // tm,),
        in_specs=[
            pl.BlockSpec((tm, D_MODEL), row),
            pl.BlockSpec((tm, ATT_WIDTH), row),
            pl.BlockSpec((tm, CONV_WIDTH), row),
            pl.BlockSpec((D_MODEL, D_MODEL), const),
            pl.BlockSpec((1, D_MODEL), const),
        ],
        out_specs=[pl.BlockSpec((tm, D_MODEL), row), pl.BlockSpec((tm, D_MODEL), row)],
        out_shape=[jax.ShapeDtypeStruct((m, D_MODEL), F32), jax.ShapeDtypeStruct((m, D_MODEL), BF16)],
        compiler_params=pltpu.CompilerParams(
            dimension_semantics=("arbitrary",), vmem_limit_bytes=VMEM_LIMIT),
        name="outproj",
    )(x, att, cz, w_out, g2)


def _ffn_kernel(x1_ref, h2_ref, wu_ref, wd_ref, *rest, groups_per_row):
    y_ref = rest[-2] if groups_per_row else rest[-1]

    @pl.when(pl.program_id(1) == 0)
    def _():
        y_ref[...] = x1_ref[...]

    def ffn_rows(rows):
        a = _dot(h2_ref[rows, :], wu_ref[...])
        g = jnp.square(jnp.maximum(a, 0.0)).astype(BF16)
        y_ref[rows, :] += _dot(g, wd_ref[...])

    if not groups_per_row:
        ffn_rows(slice(None))
        return

    q_ref, kn_ref, vn_ref, kt_ref, vt_ref, bias_ref, biasn_ref, _, o_ref = rest
    step = pl.program_id(0) * pl.num_programs(1) + pl.program_id(1)
    head_base = (step % groups_per_row) * DEC_HEAD_GROUP
    dec = _DecodeGroup(head_base, q_ref, kn_ref, vn_ref, kt_ref, vt_ref, bias_ref, biasn_ref, o_ref)
    half = h2_ref.shape[0] // 2
    dec.scores()
    ffn_rows(slice(0, half))
    dec.outputs()
    ffn_rows(slice(half, 2 * half))


def _ffn(x1, h2, w_up, w_down, decode=None, batch0=0, nbatch=0, *, tm=FFN_ROWS, tf=FFN_COLS):
    m = x1.shape[0]
    assert m % tm == 0 and D_FF % tf == 0
    nf = D_FF // tf
    row = lambda i, f: (i, 0)
    once = dict(pipeline_mode=pl.Buffered(1)) if decode is not None else {}
    in_specs = [
        pl.BlockSpec((tm, D_MODEL), row, **once),
        pl.BlockSpec((tm, D_MODEL), row, **once),
        pl.BlockSpec((D_MODEL, tf), lambda i, f: (0, f)),
        pl.BlockSpec((tf, D_MODEL), lambda i, f: (f, 0)),
    ]
    y_spec = pl.BlockSpec((tm, D_MODEL), row)
    y_shape = jax.ShapeDtypeStruct((m, D_MODEL), F32)
    if decode is None:
        return pl.pallas_call(
            functools.partial(_ffn_kernel, groups_per_row=0),
            grid=(m // tm, nf), in_specs=in_specs, out_specs=y_spec, out_shape=y_shape,
            compiler_params=pltpu.CompilerParams(
                dimension_semantics=("arbitrary", "arbitrary"), vmem_limit_bytes=VMEM_LIMIT),
            name="ffn",
        )(x1, h2, w_up, w_down)

    d_specs, d_operands, d_out_spec, d_out_shape = _decode_host_specs(
        decode, batch0, nbatch, (m // tm) * nf, lambda i, f: i * nf + f)
    return pl.pallas_call(
        functools.partial(_ffn_kernel, groups_per_row=N_HEADS // DEC_HEAD_GROUP),
        grid=(m // tm, nf),
        in_specs=in_specs + d_specs,
        out_specs=[y_spec, d_out_spec],
        out_shape=[y_shape, d_out_shape],
        compiler_params=pltpu.CompilerParams(
            dimension_semantics=("arbitrary", "arbitrary"), vmem_limit_bytes=VMEM_LIMIT_HOST),
        name="ffn_with_decode_attention",
    )(x1, h2, w_up, w_down, *d_operands)


def _head_sum_matrices():
    head_of_col = np.arange(ATT_WIDTH) // HEAD_DIM
    red = (head_of_col[:, None] == np.arange(LANES)[None, :]).astype(np.float32)
    return jnp.asarray(red, BF16), jnp.asarray(red.T, BF16)


def kernel(x_prompt, x_sample, state_k, state_v, state_conv, norm1_g, w_in, q_norm_g, k_norm_g,
           conv_w, w_out, norm2_g, w_up, w_down):
    depth = w_in.shape[0]
    assert depth == 1
    bp, sp, _ = x_prompt.shape
    nb, t, _ = x_sample.shape
    past = state_k.shape[2]
    assert bp == 1 and t == 8 and past == MAX_WINDOW and sp >= MAX_WINDOW

    red, expand = _head_sum_matrices()
    g1 = norm1_g[0][None]
    g2 = norm2_g[0][None]
    qg = jnp.tile(q_norm_g[0], N_HEADS)[None]
    kg = jnp.tile(k_norm_g[0], N_HEADS)[None]
    cw = conv_w[0]
    w_in_b = w_in[0].astype(BF16)

    xp = x_prompt[0]
    zero_conv = jnp.zeros((CONV_K - 1, CONV_WIDTH), F32)
    qp, kp, vp, czp, up, w_out_b, w_up_b, w_down_b = _inproj(
        xp, zero_conv, g1, w_in_b, qg, kg, red, expand, cw, grouped=False,
        casts=(w_out[0], w_up[0], w_down[0]))

    xs = x_sample.reshape(nb * t, D_MODEL)
    qs, ks, vs, czs, us = _inproj(xs, state_conv[0], g1, w_in_b, qg, kg, red, expand, cw, grouped=True)
    kt = jnp.transpose(state_k[0], (0, 2, 3, 1))
    vt = jnp.transpose(state_v[0], (0, 2, 3, 1))
    per_batch = lambda a: a.reshape(nb, t, ATT_WIDTH)
    decode = (per_batch(qs), per_batch(ks), per_batch(vs), kt, vt)
    gpr = N_HEADS // DEC_HEAD_GROUP
    nb_attn = (sp // SUPER) * (ATT_WIDTH // LANES) * PROMPT_QUARTERS // gpr
    assert 0 < nb_attn < nb

    attp, atts_a = _prompt_attention(qp, kp, vp, decode, 0, nb_attn)
    x1p, h2p = _outproj(xp, attp, czp, w_out_b, g2)
    yp, atts_f = _ffn(x1p, h2p, w_up_b, w_down_b, decode, nb_attn, nb - nb_attn)
    atts = jnp.concatenate([atts_a, atts_f], axis=0).reshape(nb * t, ATT_WIDTH)
    x1s, h2s = _outproj(xs, atts, czs, w_out_b, g2)
    ys = _ffn(x1s, h2s, w_up_b, w_down_b)

    keep = min(MAX_WINDOW, sp)
    new_k_prompt = kp[sp - keep:].reshape(1, 1, keep, N_HEADS, HEAD_DIM)
    new_v_prompt = vp[sp - keep:].reshape(1, 1, keep, N_HEADS, HEAD_DIM)
    new_conv_prompt = up[8 - (CONV_K - 1):].reshape(1, 1, CONV_K - 1, CONV_WIDTH)
    new_k_sample = ks.reshape(1, nb, t, N_HEADS, HEAD_DIM)
    new_v_sample = vs.reshape(1, nb, t, N_HEADS, HEAD_DIM)
    new_conv_sample = us.reshape(nb, t, CONV_WIDTH)[:, t - (CONV_K - 1):][None]
    return (yp[None], ys.reshape(nb, t, D_MODEL), new_k_prompt, new_v_prompt, new_conv_prompt,
            new_k_sample, new_v_sample, new_conv_sample)
```

```python
import functools

import numpy as np
import jax
import jax.numpy as jnp
from jax import lax
from jax.experimental import pallas as pl
from jax.experimental.pallas import tpu as pltpu

F32 = jnp.float32
BF16 = jnp.bfloat16

D_MODEL = 2048
ATT_WIDTH = 1024
CONV_WIDTH = 1024
HEAD_DIM = 64
N_HEADS = 16
WINDOWS = (128, 512, 2048)
DILATIONS = (1, 4, 16)
MAX_WINDOW = 2048
CONV_K = 3
D_FF = 4 * D_MODEL
EPS = 1e-6

LANES = 128
Q_TILE = 128
SUPER = MAX_WINDOW
VMEM_LIMIT = 56 * 1024 * 1024
VMEM_LIMIT_HOST = 60 * 1024 * 1024

INPROJ_ROWS = 512
OUTPROJ_ROWS = 512
FFN_ROWS, FFN_COLS = 1024, 512


def _dot(a, b):
    return jnp.dot(a, b, preferred_element_type=F32)


def _dot_nt(a, b):
    return lax.dot_general(a, b, (((1,), (1,)), ((), ())), preferred_element_type=F32)


def _alibi_slopes():
    return 2.0 ** (-8.0 * np.arange(1, N_HEADS + 1, dtype=np.float64) / N_HEADS)


def _head_rmsnorm(p, gain_row, red, expand):
    ss = _dot((p * p).astype(BF16), red)
    inv = lax.rsqrt(ss * (1.0 / HEAD_DIM) + EPS)
    hi = inv.astype(BF16)
    lo = (inv - hi.astype(F32)).astype(BF16)
    inv_e = _dot(jnp.concatenate([hi, lo], axis=1), expand)
    return p * inv_e * gain_row


def _inproj_kernel(*refs, tm, grouped, n_cast):
    x_ref, g1_ref, w_ref, qg_ref, kg_ref, red_ref, exp_ref, cw_ref, past_ref = refs[:9]
    cast_in = refs[9:9 + n_cast]
    q_ref, k_ref, v_ref, cz_ref, u_ref = refs[9 + n_cast:14 + n_cast]
    cast_out = refs[14 + n_cast:14 + 2 * n_cast]
    h_scr, gb_scr, c_scr, ubuf = refs[14 + 2 * n_cast:]
    i = pl.program_id(0)
    j = pl.program_id(1)

    @pl.when(j == 0)
    def _():
        x = x_ref[...]
        ms = jnp.mean(x * x, axis=-1, keepdims=True)
        h_scr[...] = (x * lax.rsqrt(ms + EPS) * g1_ref[...]).astype(BF16)

    proj = _dot(h_scr[...], w_ref[...])

    for src, dst in zip(cast_in, cast_out):
        dst[...] = src[...].astype(BF16)

    @pl.when(j == 0)
    def _():
        q_ref[...] = _head_rmsnorm(proj, qg_ref[...], red_ref[...], exp_ref[...]) * (HEAD_DIM ** -0.5)

    @pl.when(j == 1)
    def _():
        k_ref[...] = _head_rmsnorm(proj, kg_ref[...], red_ref[...], exp_ref[...])

    @pl.when(j == 2)
    def _():
        v_ref[...] = proj

    @pl.when(j == 3)
    def _():
        gb_scr[...] = proj

    @pl.when(j == 4)
    def _():
        c_scr[...] = proj

    @pl.when(j == 5)
    def _():
        u = c_scr[...] * proj
        cw = cw_ref[...]
        if grouped:
            g = tm // 8
            u3 = u.reshape(g, 8, CONV_WIDTH)
            past = past_ref[...]
            p0 = past[:, 0:1, :]
            p1 = past[:, 1:2, :]
            tok = lax.broadcasted_iota(jnp.int32, u3.shape, 1)
            prev1 = jnp.where(tok == 0, p1, pltpu.roll(u3, 1, axis=1))
            prev2 = jnp.where(tok == 0, p0, jnp.where(tok == 1, p1, pltpu.roll(u3, 2, axis=1)))
            conv = cw[0:1][None] * prev2 + cw[1:2][None] * prev1 + cw[2:3][None] * u3
            cz_ref[...] = (gb_scr[...] * conv.reshape(tm, CONV_WIDTH)).astype(BF16)
            u_ref[...] = u
        else:
            @pl.when(i == 0)
            def _():
                ubuf[0:8, :] = jnp.zeros((8, CONV_WIDTH), F32)
                ubuf[6:8, :] = past_ref[...]

            ubuf[8:tm + 8, :] = u
            conv = cw[0:1] * ubuf[6:tm + 6, :] + cw[1:2] * ubuf[7:tm + 7, :] + cw[2:3] * u
            cz_ref[...] = (gb_scr[...] * conv).astype(BF16)
            tail = ubuf[tm:tm + 8, :]
            ubuf[0:8, :] = tail
            u_ref[...] = tail


CAST_SLABS = 64


def _inproj(x, past, g1, w_in, qg, kg, red, expand, cw, *, grouped, casts=(), tm=INPROJ_ROWS):
    m = x.shape[0]
    assert m % tm == 0
    nt = m // tm
    row = lambda i, j: (i, 0)
    const2 = lambda i, j: (0, 0)
    assert not casts or nt * 6 >= CAST_SLABS
    slab = lambda i, j: (jnp.minimum(i * 6 + j, CAST_SLABS - 1), 0)
    cast_specs = [pl.BlockSpec((a.shape[0] // CAST_SLABS, a.shape[1]), slab) for a in casts]
    cast_shapes = [jax.ShapeDtypeStruct(a.shape, BF16) for a in casts]
    assert all(a.shape[0] % (16 * CAST_SLABS) == 0 for a in casts)
    if grouped:
        past_spec = pl.BlockSpec((tm // 8, CONV_K - 1, CONV_WIDTH), lambda i, j: (i, 0, 0))
        u_shape = jax.ShapeDtypeStruct((m, CONV_WIDTH), F32)
        u_spec = pl.BlockSpec((tm, CONV_WIDTH), row)
    else:
        past_spec = pl.BlockSpec((CONV_K - 1, CONV_WIDTH), const2)
        u_shape = jax.ShapeDtypeStruct((8, CONV_WIDTH), F32)
        u_spec = pl.BlockSpec((8, CONV_WIDTH), const2)
    out_f32 = jax.ShapeDtypeStruct((m, ATT_WIDTH), F32)
    return pl.pallas_call(
        functools.partial(_inproj_kernel, tm=tm, grouped=grouped, n_cast=len(casts)),
        grid=(nt, 6),
        in_specs=[
            pl.BlockSpec((tm, D_MODEL), row),
            pl.BlockSpec((1, D_MODEL), const2),
            pl.BlockSpec((D_MODEL, 1024), lambda i, j: (0, j)),
            pl.BlockSpec((1, ATT_WIDTH), const2),
            pl.BlockSpec((1, ATT_WIDTH), const2),
            pl.BlockSpec((ATT_WIDTH, LANES), const2),
            pl.BlockSpec((2 * LANES, ATT_WIDTH), const2),
            pl.BlockSpec((CONV_K, CONV_WIDTH), const2),
            past_spec,
        ] + cast_specs,
        out_specs=[
            pl.BlockSpec((tm, ATT_WIDTH), row),
            pl.BlockSpec((tm, ATT_WIDTH), row),
            pl.BlockSpec((tm, ATT_WIDTH), row),
            pl.BlockSpec((tm, CONV_WIDTH), row),
            u_spec,
        ] + cast_specs,
        out_shape=[out_f32, out_f32, out_f32,
                   jax.ShapeDtypeStruct((m, CONV_WIDTH), BF16), u_shape] + cast_shapes,
        scratch_shapes=[
            pltpu.VMEM((tm, D_MODEL), BF16),
            pltpu.VMEM((tm, CONV_WIDTH), F32),
            pltpu.VMEM((tm, CONV_WIDTH), F32),
            pltpu.VMEM((tm + 8, CONV_WIDTH), F32),
        ],
        compiler_params=pltpu.CompilerParams(
            dimension_semantics=("arbitrary", "arbitrary"), vmem_limit_bytes=VMEM_LIMIT),
        name="inproj_grouped" if grouped else "inproj_seq",
    )(x, g1, w_in, qg, kg, red, expand, cw, past, *casts)


def _prompt_bias_table():
    qi = np.arange(Q_TILE)[:, None]
    kc = np.arange(2 * Q_TILE)[None, :]
    jdist = (Q_TILE + qi - kc).astype(np.float64)
    valid = (jdist >= 0) & (jdist <= Q_TILE)
    slopes = _alibi_slopes()
    tab = np.empty((2, len(DILATIONS), N_HEADS, Q_TILE, 2 * Q_TILE), np.float32)
    for f in range(2):
        ok = valid & (kc >= Q_TILE) if f else valid
        for p, d in enumerate(DILATIONS):
            for h in range(N_HEADS):
                tab[f, p, h] = np.where(ok, -slopes[h] * d * jdist, -np.inf)
    return tab


def _prompt_attn_kernel(q_ref, kc_ref, kp_ref, vc_ref, vp_ref, bias_ref,
                        dq_ref, dkn_ref, dvn_ref, dkt_ref, dvt_ref, dbias_ref, dbiasn_ref,
                        o_ref, do_ref, o_scr, m_scr, l_scr):
    first = jnp.where(pl.program_id(0) == 0, 1, 0)
    quarter = pl.program_id(2)
    lane = lax.broadcasted_iota(jnp.int32, (Q_TILE, LANES), 1)
    lo = lane < HEAD_DIM
    hi = jnp.logical_not(lo)

    def attend(units):
        def scores(unit):
            pat, variant, rows, kprev_ref, vprev_ref, prev = unit
            q = q_ref[rows, :]
            k2 = jnp.concatenate([kprev_ref[prev, :], kc_ref[rows, :]], axis=0).astype(BF16)
            q2 = jnp.concatenate([jnp.where(lo, q, 0.0), jnp.where(hi, q, 0.0)], axis=0).astype(BF16)
            return _dot_nt(q2, k2) + bias_ref[variant, pat].reshape(2 * Q_TILE, 2 * Q_TILE)

        def softmax(s):
            m = jnp.max(s, axis=1, keepdims=True)
            return m, jnp.exp(s - m).astype(BF16)

        def output(unit, stats):
            pat, _, rows, _, vprev_ref, prev = unit
            v2 = jnp.concatenate([vprev_ref[prev, :], vc_ref[rows, :]], axis=0).astype(BF16)
            v_ext = jnp.concatenate([v2, jnp.ones_like(v2)], axis=1)
            m, p = stats
            f = _dot(p, v_ext)
            o_scr[pat, rows, :] = jnp.where(lo, f[:Q_TILE, :LANES], f[Q_TILE:, :LANES])
            m_scr[pat, rows, :] = jnp.where(lo, m[:Q_TILE], m[Q_TILE:])
            l_scr[pat, rows, :] = jnp.where(lo, f[:Q_TILE, LANES:], f[Q_TILE:, LANES:])

        all_scores = [scores(unit) for unit in units]
        all_stats = [softmax(ss) for ss in all_scores]
        for unit, stats in zip(units, all_stats):
            output(unit, stats)

    def unit16(r):
        rows = pl.ds(r, Q_TILE, stride=16)
        return (2, first, rows, kp_ref, vp_ref, rows)

    def unit4(start, from_prev_block):
        rows = pl.ds(start, Q_TILE, stride=4)
        if from_prev_block:
            return (1, first, rows, kp_ref, vp_ref, pl.ds(SUPER - 512 + start, Q_TILE, stride=4))
        return (1, 0, rows, kc_ref, vc_ref, pl.ds(start - 512, Q_TILE, stride=4))

    def unit1(start, from_prev_block):
        rows = pl.ds(start, Q_TILE)
        if from_prev_block:
            return (0, first, rows, kp_ref, vp_ref, pl.ds(SUPER - Q_TILE, Q_TILE))
        return (0, 0, rows, kc_ref, vc_ref, pl.ds(start - Q_TILE, Q_TILE))

    def pat16(g):
        return [unit16(g * 4 + u) for u in range(4)]

    first4 = [unit4(r4, True) for r4 in range(4)]
    rest4 = [unit4(512 * c + r4, False) for c in range(1, SUPER // 512) for r4 in range(4)]

    def pat1(g):
        return [unit1((1 + 5 * g + u) * Q_TILE, False) for u in range(5)]

    def merge(c, carry):
        rows = pl.ds(pl.multiple_of(c * 256, 256), 256)
        ms = [m_scr[p, rows, :] for p in range(3)]
        mx = jnp.maximum(jnp.maximum(ms[0], ms[1]), ms[2])
        num = jnp.zeros((256, LANES), F32)
        den = jnp.zeros((256, LANES), F32)
        for p in range(3):
            w = jnp.exp(ms[p] - mx)
            num = num + w * o_scr[p, rows, :]
            den = den + w * l_scr[p, rows, :]
        o_ref[rows, :] = (num / den).astype(BF16)
        return carry

    step = (pl.program_id(0) * pl.num_programs(1) + pl.program_id(1)) * pl.num_programs(2) + quarter
    head_base = (step % (N_HEADS // DEC_HEAD_GROUP)) * DEC_HEAD_GROUP

    def quarter_body(groups):
        dec = _DecodeGroup(head_base, dq_ref, dkn_ref, dvn_ref, dkt_ref, dvt_ref, dbias_ref, dbiasn_ref,
                           do_ref)
        dec.scores()
        attend(groups[0])
        dec.outputs()
        for g in groups[1:]:
            attend(g)

    @pl.when(quarter == 0)
    def _():
        quarter_body([pat16(0), pat16(1), pat1(0)])

    @pl.when(quarter == 1)
    def _():
        quarter_body([pat16(2), pat16(3), pat1(1)])

    @pl.when(quarter == 2)
    def _():
        quarter_body([first4 + [unit1(0, True)], rest4[:6], rest4[6:]])

    @pl.when(quarter == 3)
    def _():
        quarter_body([pat1(2)])
        lax.fori_loop(0, SUPER // 256, merge, 0)


PROMPT_QUARTERS = 4


def _prompt_attention(q, k, v, decode, batch0, nbatch):
    s = q.shape[0]
    assert s % SUPER == 0
    bias = jnp.asarray(_prompt_bias_table())
    cur = lambda b, p, c: (b, p)
    prev = lambda b, p, c: (jnp.maximum(b - 1, 0), p)
    blk = (SUPER, LANES)
    npair = ATT_WIDTH // LANES
    grid = (s // SUPER, npair, PROMPT_QUARTERS)
    step_of = lambda b, p, c: (b * npair + p) * PROMPT_QUARTERS + c
    d_specs, d_operands, d_out_spec, d_out_shape = _decode_host_specs(
        decode, batch0, nbatch, grid[0] * grid[1] * grid[2], step_of)
    return pl.pallas_call(
        _prompt_attn_kernel,
        grid=grid,
        in_specs=[
            pl.BlockSpec(blk, cur),
            pl.BlockSpec(blk, cur),
            pl.BlockSpec(blk, prev),
            pl.BlockSpec(blk, cur),
            pl.BlockSpec(blk, prev),
            pl.BlockSpec((2, 3, 2, Q_TILE, 2 * Q_TILE), lambda b, p, c: (0, 0, p, 0, 0)),
        ] + d_specs,
        out_specs=[pl.BlockSpec(blk, cur), d_out_spec],
        out_shape=[jax.ShapeDtypeStruct((s, ATT_WIDTH), BF16), d_out_shape],
        scratch_shapes=[pltpu.VMEM((3, SUPER, LANES), F32)] * 3,
        compiler_params=pltpu.CompilerParams(
            dimension_semantics=("arbitrary", "arbitrary", "arbitrary"), vmem_limit_bytes=VMEM_LIMIT),
        name="prompt_attention_with_decode_attention",
    )(q, k, k, v, v, bias, *d_operands)


DEC_SEQ = 8
DEC_HEAD_GROUP = 8


def _decode_bias_tables(past_len):
    dist_max = MAX_WINDOW
    mult = np.zeros(dist_max + 1, np.float64)
    for w, d in zip(WINDOWS, DILATIONS):
        mult[np.arange(w // d + 1) * d] += 1.0
    slopes = _alibi_slopes()

    def table(dist):
        ok = (dist >= 0) & (dist <= dist_max)
        dc = np.clip(dist, 0, dist_max)
        mu = np.where(ok, mult[dc], 0.0)
        with np.errstate(divide="ignore"):
            logm = np.where(mu > 0, np.log(np.maximum(mu, 1e-30)), -np.inf)
        return (-slopes[:, None, None] * dc[None].astype(np.float64) + logm[None]).astype(np.float32)

    qpos = past_len + np.arange(DEC_SEQ)
    old = table(qpos[:, None] - np.arange(past_len)[None, :])
    new = np.full((N_HEADS, DEC_SEQ, 2 * DEC_SEQ), -np.inf, np.float32)
    new[:, :, :DEC_SEQ] = table(qpos[:, None] - qpos[None, :])
    return old, new


def _pad16(a):
    return jnp.concatenate([a, jnp.zeros_like(a)], axis=0).astype(BF16)


class _DecodeGroup:
    def __init__(self, head_base, q_ref, kn_ref, vn_ref, kt_ref, vt_ref, bias_ref, biasn_ref, o_ref):
        self.head_base = head_base
        self.refs = (q_ref, kn_ref, vn_ref, kt_ref, vt_ref, bias_ref, biasn_ref, o_ref)

    def scores(self):
        q_ref, kn_ref, _, kt_ref, _, bias_ref, biasn_ref, _ = self.refs
        self.s = []
        for u in range(DEC_HEAD_GROUP):
            c = slice(u * HEAD_DIM, (u + 1) * HEAD_DIM)
            q = _pad16(q_ref[:, c])
            s = _dot(q, kt_ref[u].astype(BF16))[:DEC_SEQ] + bias_ref[self.head_base + u]
            sn = _dot_nt(q, _pad16(kn_ref[:, c]))[:DEC_SEQ] + biasn_ref[self.head_base + u]
            self.s.append((s, sn))

    def outputs(self):
        _, _, vn_ref, _, vt_ref, _, _, o_ref = self.refs
        probs = []
        for s, sn in self.s:
            m = jnp.maximum(jnp.max(s, axis=1, keepdims=True), jnp.max(sn, axis=1, keepdims=True))
            probs.append((jnp.exp(s - m), jnp.exp(sn - m)))
        for u, (p, pn) in enumerate(probs):
            c = slice(u * HEAD_DIM, (u + 1) * HEAD_DIM)
            l = jnp.sum(p, axis=1, keepdims=True) + jnp.sum(pn, axis=1, keepdims=True)
            o = (_dot_nt(_pad16(p), vt_ref[u].astype(BF16))[:DEC_SEQ]
                 + _dot(_pad16(pn), _pad16(vn_ref[:, c]))[:DEC_SEQ])
            o_ref[:, c] = o / l


def _decode_host_specs(decode, batch0, nbatch, nsteps, step_of):
    q, kn, vn, kt, vt = decode
    nb, _, _, past = kt.shape
    gpr = N_HEADS // DEC_HEAD_GROUP
    gw = DEC_HEAD_GROUP * HEAD_DIM
    assert q.shape == (nb, DEC_SEQ, ATT_WIDTH) and nsteps == nbatch * gpr and batch0 + nbatch <= nb
    old, new = _decode_bias_tables(past)
    small = pl.BlockSpec((None, DEC_SEQ, gw),
                         lambda *g: (batch0 + step_of(*g) // gpr, 0, step_of(*g) % gpr))
    big = pl.BlockSpec((None, DEC_HEAD_GROUP, HEAD_DIM, past),
                       lambda *g: (batch0 + step_of(*g) // gpr, step_of(*g) % gpr, 0, 0))
    const3 = lambda *g: (0, 0, 0)
    in_specs = [small, small, small, big, big,
                pl.BlockSpec((N_HEADS, DEC_SEQ, past), const3),
                pl.BlockSpec((N_HEADS, DEC_SEQ, 2 * DEC_SEQ), const3)]
    out_spec = pl.BlockSpec((None, DEC_SEQ, gw), lambda *g: (step_of(*g) // gpr, 0, step_of(*g) % gpr))
    out_shape = jax.ShapeDtypeStruct((nbatch, DEC_SEQ, ATT_WIDTH), F32)
    return in_specs, (q, kn, vn, kt, vt, jnp.asarray(old), jnp.asarray(new)), out_spec, out_shape


def _outproj_kernel(x_ref, att_ref, cz_ref, wo_ref, g2_ref, x1_ref, h2_ref):
    mixed = jnp.concatenate([att_ref[...].astype(BF16), cz_ref[...]], axis=1)
    y = x_ref[...] + _dot(mixed, wo_ref[...])
    x1_ref[...] = y
    ms = jnp.mean(y * y, axis=-1, keepdims=True)
    h2_ref[...] = (y * lax.rsqrt(ms + EPS) * g2_ref[...]).astype(BF16)


def _outproj(x, att, cz, w_out, g2, *, tm=OUTPROJ_ROWS):
    m = x.shape[0]
    assert m % tm == 0
    row = lambda i: (i, 0)
    const = lambda i: (0, 0)
    return pl.pallas_call(
        _outproj_kernel,
        grid=(m // tm,),
        in_specs=[
            pl.BlockSpec((tm, D_MODEL), row),
            pl.BlockSpec((tm, ATT_WIDTH), row),
            pl.BlockSpec((tm, CONV_WIDTH), row),
            pl.BlockSpec((D_MODEL, D_MODEL), const),
            pl.BlockSpec((1, D_MODEL), const),
        ],
        out_specs=[pl.BlockSpec((tm, D_MODEL), row), pl.BlockSpec((tm, D_MODEL), row)],
        out_shape=[jax.ShapeDtypeStruct((m, D_MODEL), F32), jax.ShapeDtypeStruct((m, D_MODEL), BF16)],
        compiler_params=pltpu.CompilerParams(
            dimension_semantics=("arbitrary",), vmem_limit_bytes=VMEM_LIMIT),
        name="outproj",
    )(x, att, cz, w_out, g2)


def _ffn_kernel(x1_ref, h2_ref, wu_ref, wd_ref, *rest, groups_per_row):
    y_ref = rest[-2] if groups_per_row else rest[-1]

    @pl.when(pl.program_id(1) == 0)
    def _():
        y_ref[...] = x1_ref[...]

    def ffn_rows(rows):
        a = _dot(h2_ref[rows, :], wu_ref[...])
        g = jnp.square(jnp.maximum(a, 0.0)).astype(BF16)
        y_ref[rows, :] += _dot(g, wd_ref[...])

    if not groups_per_row:
        ffn_rows(slice(None))
        return

    q_ref, kn_ref, vn_ref, kt_ref, vt_ref, bias_ref, biasn_ref, _, o_ref = rest
    step = pl.program_id(0) * pl.num_programs(1) + pl.program_id(1)
    head_base = (step % groups_per_row) * DEC_HEAD_GROUP
    dec = _DecodeGroup(head_base, q_ref, kn_ref, vn_ref, kt_ref, vt_ref, bias_ref, biasn_ref, o_ref)
    half = h2_ref.shape[0] // 2
    dec.scores()
    ffn_rows(slice(0, half))
    dec.outputs()
    ffn_rows(slice(half, 2 * half))


def _ffn(x1, h2, w_up, w_down, decode=None, batch0=0, nbatch=0, *, tm=FFN_ROWS, tf=FFN_COLS):
    m = x1.shape[0]
    assert m % tm == 0 and D_FF % tf == 0
    nf = D_FF // tf
    row = lambda i, f: (i, 0)
    once = dict(pipeline_mode=pl.Buffered(1)) if decode is not None else {}
    in_specs = [
        pl.BlockSpec((tm, D_MODEL), row, **once),
        pl.BlockSpec((tm, D_MODEL), row, **once),
        pl.BlockSpec((D_MODEL, tf), lambda i, f: (0, f)),
        pl.BlockSpec((tf, D_MODEL), lambda i, f: (f, 0)),
    ]
    y_spec = pl.BlockSpec((tm, D_MODEL), row)
    y_shape = jax.ShapeDtypeStruct((m, D_MODEL), F32)
    if decode is None:
        return pl.pallas_call(
            functools.partial(_ffn_kernel, groups_per_row=0),
            grid=(m // tm, nf), in_specs=in_specs, out_specs=y_spec, out_shape=y_shape,
            compiler_params=pltpu.CompilerParams(
                dimension_semantics=("arbitrary", "arbitrary"), vmem_limit_bytes=VMEM_LIMIT),
            name="ffn",
        )(x1, h2, w_up, w_down)

    d_specs, d_operands, d_out_spec, d_out_shape = _decode_host_specs(
        decode, batch0, nbatch, (m // tm) * nf, lambda i, f: i * nf + f)
    return pl.pallas_call(
        functools.partial(_ffn_kernel, groups_per_row=N_HEADS // DEC_HEAD_GROUP),
        grid=(m // tm, nf),
        in_specs=in_specs + d_specs,
        out_specs=[y_spec, d_out_spec],
        out_shape=[y_shape, d_out_shape],
        compiler_params=pltpu.CompilerParams(
            dimension_semantics=("arbitrary", "arbitrary"), vmem_limit_bytes=VMEM_LIMIT_HOST),
        name="ffn_with_decode_attention",
    )(x1, h2, w_up, w_down, *d_operands)


def _head_sum_matrices():
    head_of_col = np.arange(ATT_WIDTH) // HEAD_DIM
    red = (head_of_col[:, None] == np.arange(LANES)[None, :]).astype(np.float32)
    return jnp.asarray(red, BF16), jnp.asarray(np.concatenate([red.T, red.T], axis=0), BF16)


def kernel(x_prompt, x_sample, state_k, state_v, state_conv, norm1_g, w_in, q_norm_g, k_norm_g,
           conv_w, w_out, norm2_g, w_up, w_down):
    depth = w_in.shape[0]
    assert depth == 1
    bp, sp, _ = x_prompt.shape
    nb, t, _ = x_sample.shape
    past = state_k.shape[2]
    assert bp == 1 and t == 8 and past == MAX_WINDOW and sp >= MAX_WINDOW

    red, expand = _head_sum_matrices()
    g1 = norm1_g[0][None]
    g2 = norm2_g[0][None]
    qg = jnp.tile(q_norm_g[0], N_HEADS)[None]
    kg = jnp.tile(k_norm_g[0], N_HEADS)[None]
    cw = conv_w[0]
    w_in_b = w_in[0].astype(BF16)

    xp = x_prompt[0]
    zero_conv = jnp.zeros((CONV_K - 1, CONV_WIDTH), F32)
    qp, kp, vp, czp, up, w_out_b, w_up_b, w_down_b = _inproj(
        xp, zero_conv, g1, w_in_b, qg, kg, red, expand, cw, grouped=False,
        casts=(w_out[0], w_up[0], w_down[0]))

    xs = x_sample.reshape(nb * t, D_MODEL)
    qs, ks, vs, czs, us = _inproj(xs, state_conv[0], g1, w_in_b, qg, kg, red, expand, cw, grouped=True)
    kt = jnp.transpose(state_k[0], (0, 2, 3, 1))
    vt = jnp.transpose(state_v[0], (0, 2, 3, 1))
    per_batch = lambda a: a.reshape(nb, t, ATT_WIDTH)
    decode = (per_batch(qs), per_batch(ks), per_batch(vs), kt, vt)
    gpr = N_HEADS // DEC_HEAD_GROUP
    nb_attn = (sp // SUPER) * (ATT_WIDTH // LANES) * PROMPT_QUARTERS // gpr
    assert 0 < nb_attn < nb

    attp, atts_a = _prompt_attention(qp, kp, vp, decode, 0, nb_attn)
    x1p, h2p = _outproj(xp, attp, czp, w_out_b, g2)
    yp, atts_f = _ffn(x1p, h2p, w_up_b, w_down_b, decode, nb_attn, nb - nb_attn)
    atts = jnp.concatenate([atts_a, atts_f], axis=0).reshape(nb * t, ATT_WIDTH)
    x1s, h2s = _outproj(xs, atts, czs, w_out_b, g2)
    ys = _ffn(x1s, h2s, w_up_b, w_down_b)

    keep = min(MAX_WINDOW, sp)
    new_k_prompt = kp[sp - keep:].reshape(1, 1, keep, N_HEADS, HEAD_DIM)
    new_v_prompt = vp[sp - keep:].reshape(1, 1, keep, N_HEADS, HEAD_DIM)
    new_conv_prompt = up[8 - (CONV_K - 1):].reshape(1, 1, CONV_K - 1, CONV_WIDTH)
    new_k_sample = ks.reshape(1, nb, t, N_HEADS, HEAD_DIM)
    new_v_sample = vs.reshape(1, nb, t, N_HEADS, HEAD_DIM)
    new_conv_sample = us.reshape(nb, t, CONV_WIDTH)[:, t - (CONV_K - 1):][None]
    return (yp[None], ys.reshape(nb, t, D_MODEL), new_k_prompt, new_v_prompt, new_conv_prompt,
            new_k_sample, new_v_sample, new_conv_sample)
```

```python
import functools

import numpy as np
import jax
import jax.numpy as jnp
from jax import lax
from jax.experimental import pallas as pl
from jax.experimental.pallas import tpu as pltpu

F32 = jnp.float32
BF16 = jnp.bfloat16

D_MODEL = 2048
ATT_WIDTH = 1024
CONV_WIDTH = 1024
HEAD_DIM = 64
N_HEADS = 16
WINDOWS = (128, 512, 2048)
DILATIONS = (1, 4, 16)
MAX_WINDOW = 2048
CONV_K = 3
D_FF = 4 * D_MODEL
EPS = 1e-6
LOG2E = float(np.log2(np.e))
Q_SCALE = HEAD_DIM ** -0.5 * LOG2E

LANES = 128
Q_TILE = 128
SUPER = MAX_WINDOW
VMEM_LIMIT = 56 * 1024 * 1024
VMEM_LIMIT_HOST = 60 * 1024 * 1024

INPROJ_ROWS = 512
OUTPROJ_ROWS = 512
FFN_ROWS, FFN_COLS = 1024, 512


def _dot(a, b):
    return jnp.dot(a, b, preferred_element_type=F32)


def _dot_nt(a, b):
    return lax.dot_general(a, b, (((1,), (1,)), ((), ())), preferred_element_type=F32)


def _alibi_slopes():
    return 2.0 ** (-8.0 * np.arange(1, N_HEADS + 1, dtype=np.float64) / N_HEADS)


def _head_rmsnorm(p, gain_row, red, expand):
    ss = _dot((p * p).astype(BF16), red)
    inv = lax.rsqrt(ss * (1.0 / HEAD_DIM) + EPS)
    hi = inv.astype(BF16)
    lo = (inv - hi.astype(F32)).astype(BF16)
    inv_e = _dot(jnp.concatenate([hi, lo], axis=1), expand)
    return p * inv_e * gain_row


def _inproj_kernel(*refs, tm, grouped, n_cast):
    x_ref, g1_ref, w_ref, qg_ref, kg_ref, red_ref, exp_ref, cw_ref, past_ref = refs[:9]
    cast_in = refs[9:9 + n_cast]
    q_ref, k_ref, v_ref, cz_ref, u_ref = refs[9 + n_cast:14 + n_cast]
    cast_out = refs[14 + n_cast:14 + 2 * n_cast]
    h_scr, gb_scr, c_scr, ubuf = refs[14 + 2 * n_cast:]
    i = pl.program_id(0)
    j = pl.program_id(1)

    @pl.when(j == 0)
    def _():
        x = x_ref[...]
        ms = jnp.mean(x * x, axis=-1, keepdims=True)
        h_scr[...] = (x * lax.rsqrt(ms + EPS) * g1_ref[...]).astype(BF16)

    proj = _dot(h_scr[...], w_ref[...])

    for src, dst in zip(cast_in, cast_out):
        dst[...] = src[...].astype(BF16)

    @pl.when(j == 0)
    def _():
        q_ref[...] = _head_rmsnorm(proj, qg_ref[...], red_ref[...], exp_ref[...]) * Q_SCALE

    @pl.when(j == 1)
    def _():
        k_ref[...] = _head_rmsnorm(proj, kg_ref[...], red_ref[...], exp_ref[...])

    @pl.when(j == 2)
    def _():
        v_ref[...] = proj

    @pl.when(j == 3)
    def _():
        gb_scr[...] = proj

    @pl.when(j == 4)
    def _():
        c_scr[...] = proj

    @pl.when(j == 5)
    def _():
        u = c_scr[...] * proj
        cw = cw_ref[...]
        if grouped:
            g = tm // 8
            u3 = u.reshape(g, 8, CONV_WIDTH)
            past = past_ref[...]
            p0 = past[:, 0:1, :]
            p1 = past[:, 1:2, :]
            tok = lax.broadcasted_iota(jnp.int32, u3.shape, 1)
            prev1 = jnp.where(tok == 0, p1, pltpu.roll(u3, 1, axis=1))
            prev2 = jnp.where(tok == 0, p0, jnp.where(tok == 1, p1, pltpu.roll(u3, 2, axis=1)))
            conv = cw[0:1][None] * prev2 + cw[1:2][None] * prev1 + cw[2:3][None] * u3
            cz_ref[...] = (gb_scr[...] * conv.reshape(tm, CONV_WIDTH)).astype(BF16)
            u_ref[...] = u
        else:
            @pl.when(i == 0)
            def _():
                ubuf[0:8, :] = jnp.zeros((8, CONV_WIDTH), F32)
                ubuf[6:8, :] = past_ref[...]

            ubuf[8:tm + 8, :] = u
            conv = cw[0:1] * ubuf[6:tm + 6, :] + cw[1:2] * ubuf[7:tm + 7, :] + cw[2:3] * u
            cz_ref[...] = (gb_scr[...] * conv).astype(BF16)
            tail = ubuf[tm:tm + 8, :]
            ubuf[0:8, :] = tail
            u_ref[...] = tail


CAST_SLABS = 64


def _inproj(x, past, g1, w_in, qg, kg, red, expand, cw, *, grouped, casts=(), tm=INPROJ_ROWS):
    m = x.shape[0]
    assert m % tm == 0
    nt = m // tm
    row = lambda i, j: (i, 0)
    const2 = lambda i, j: (0, 0)
    assert not casts or nt * 6 >= CAST_SLABS
    slab = lambda i, j: (jnp.minimum(i * 6 + j, CAST_SLABS - 1), 0)
    cast_specs = [pl.BlockSpec((a.shape[0] // CAST_SLABS, a.shape[1]), slab) for a in casts]
    cast_shapes = [jax.ShapeDtypeStruct(a.shape, BF16) for a in casts]
    assert all(a.shape[0] % (16 * CAST_SLABS) == 0 for a in casts)
    if grouped:
        past_spec = pl.BlockSpec((tm // 8, CONV_K - 1, CONV_WIDTH), lambda i, j: (i, 0, 0))
        u_shape = jax.ShapeDtypeStruct((m, CONV_WIDTH), F32)
        u_spec = pl.BlockSpec((tm, CONV_WIDTH), row)
    else:
        past_spec = pl.BlockSpec((CONV_K - 1, CONV_WIDTH), const2)
        u_shape = jax.ShapeDtypeStruct((8, CONV_WIDTH), F32)
        u_spec = pl.BlockSpec((8, CONV_WIDTH), const2)
    out_f32 = jax.ShapeDtypeStruct((m, ATT_WIDTH), F32)
    return pl.pallas_call(
        functools.partial(_inproj_kernel, tm=tm, grouped=grouped, n_cast=len(casts)),
        grid=(nt, 6),
        in_specs=[
            pl.BlockSpec((tm, D_MODEL), row),
            pl.BlockSpec((1, D_MODEL), const2),
            pl.BlockSpec((D_MODEL, 1024), lambda i, j: (0, j)),
            pl.BlockSpec((1, ATT_WIDTH), const2),
            pl.BlockSpec((1, ATT_WIDTH), const2),
            pl.BlockSpec((ATT_WIDTH, LANES), const2),
            pl.BlockSpec((2 * LANES, ATT_WIDTH), const2),
            pl.BlockSpec((CONV_K, CONV_WIDTH), const2),
            past_spec,
        ] + cast_specs,
        out_specs=[
            pl.BlockSpec((tm, ATT_WIDTH), row),
            pl.BlockSpec((tm, ATT_WIDTH), row),
            pl.BlockSpec((tm, ATT_WIDTH), row),
            pl.BlockSpec((tm, CONV_WIDTH), row),
            u_spec,
        ] + cast_specs,
        out_shape=[out_f32, out_f32, out_f32,
                   jax.ShapeDtypeStruct((m, CONV_WIDTH), BF16), u_shape] + cast_shapes,
        scratch_shapes=[
            pltpu.VMEM((tm, D_MODEL), BF16),
            pltpu.VMEM((tm, CONV_WIDTH), F32),
            pltpu.VMEM((tm, CONV_WIDTH), F32),
            pltpu.VMEM((tm + 8, CONV_WIDTH), F32),
        ],
        compiler_params=pltpu.CompilerParams(
            dimension_semantics=("arbitrary", "arbitrary"), vmem_limit_bytes=VMEM_LIMIT),
        name="inproj_grouped" if grouped else "inproj_seq",
    )(x, g1, w_in, qg, kg, red, expand, cw, past, *casts)


def _prompt_bias_table():
    qi = np.arange(Q_TILE)[:, None]
    kc = np.arange(2 * Q_TILE)[None, :]
    jdist = (Q_TILE + qi - kc).astype(np.float64)
    valid = (jdist >= 0) & (jdist <= Q_TILE)
    slopes = _alibi_slopes()
    tab = np.empty((2, len(DILATIONS), N_HEADS, Q_TILE, 2 * Q_TILE), np.float32)
    for f in range(2):
        ok = valid & (kc >= Q_TILE) if f else valid
        for p, d in enumerate(DILATIONS):
            for h in range(N_HEADS):
                tab[f, p, h] = np.where(ok, -slopes[h] * d * jdist * LOG2E, -np.inf)
    return tab


def _prompt_attn_kernel(q_ref, kc_ref, kp_ref, vc_ref, vp_ref, bias_ref,
                        dq_ref, dkn_ref, dvn_ref, dkt_ref, dvt_ref, dbias_ref, dbiasn_ref,
                        o_ref, do_ref, o_scr, m_scr, l_scr):
    first = jnp.where(pl.program_id(0) == 0, 1, 0)
    quarter = pl.program_id(2)
    lane = lax.broadcasted_iota(jnp.int32, (Q_TILE, LANES), 1)
    lo = lane < HEAD_DIM
    hi = jnp.logical_not(lo)

    def attend(units):
        def scores(unit):
            pat, variant, rows, kprev_ref, vprev_ref, prev = unit
            q = q_ref[rows, :]
            k2 = jnp.concatenate([kprev_ref[prev, :], kc_ref[rows, :]], axis=0).astype(BF16)
            q2 = jnp.concatenate([jnp.where(lo, q, 0.0), jnp.where(hi, q, 0.0)], axis=0).astype(BF16)
            return _dot_nt(q2, k2) + bias_ref[variant, pat].reshape(2 * Q_TILE, 2 * Q_TILE)

        def softmax(s):
            m = jnp.max(s, axis=1, keepdims=True)
            return m, jnp.exp2(s - m).astype(BF16)

        def output(unit, stats):
            pat, _, rows, _, vprev_ref, prev = unit
            v2 = jnp.concatenate([vprev_ref[prev, :], vc_ref[rows, :]], axis=0).astype(BF16)
            v_ext = jnp.concatenate([v2, jnp.ones_like(v2)], axis=1)
            m, p = stats
            f = _dot(p, v_ext)
            o_scr[pat, rows, :] = jnp.where(lo, f[:Q_TILE, :LANES], f[Q_TILE:, :LANES])
            m_scr[pat, rows, :] = jnp.where(lo, m[:Q_TILE], m[Q_TILE:])
            l_scr[pat, rows, :] = jnp.where(lo, f[:Q_TILE, LANES:], f[Q_TILE:, LANES:])

        all_scores = [scores(unit) for unit in units]
        all_stats = [softmax(ss) for ss in all_scores]
        for unit, stats in zip(units, all_stats):
            output(unit, stats)

    def unit16(r):
        rows = pl.ds(r, Q_TILE, stride=16)
        return (2, first, rows, kp_ref, vp_ref, rows)

    def unit4(start, from_prev_block):
        rows = pl.ds(start, Q_TILE, stride=4)
        if from_prev_block:
            return (1, first, rows, kp_ref, vp_ref, pl.ds(SUPER - 512 + start, Q_TILE, stride=4))
        return (1, 0, rows, kc_ref, vc_ref, pl.ds(start - 512, Q_TILE, stride=4))

    def unit1(start, from_prev_block):
        rows = pl.ds(start, Q_TILE)
        if from_prev_block:
            return (0, first, rows, kp_ref, vp_ref, pl.ds(SUPER - Q_TILE, Q_TILE))
        return (0, 0, rows, kc_ref, vc_ref, pl.ds(start - Q_TILE, Q_TILE))

    def pat16(g):
        return [unit16(g * 4 + u) for u in range(4)]

    first4 = [unit4(r4, True) for r4 in range(4)]
    rest4 = [unit4(512 * c + r4, False) for c in range(1, SUPER // 512) for r4 in range(4)]

    def pat1(g):
        return [unit1((1 + 5 * g + u) * Q_TILE, False) for u in range(5)]

    def merge(c, carry):
        rows = pl.ds(pl.multiple_of(c * 256, 256), 256)
        ms = [m_scr[p, rows, :] for p in range(3)]
        mx = jnp.maximum(jnp.maximum(ms[0], ms[1]), ms[2])
        num = jnp.zeros((256, LANES), F32)
        den = jnp.zeros((256, LANES), F32)
        for p in range(3):
            w = jnp.exp2(ms[p] - mx)
            num = num + w * o_scr[p, rows, :]
            den = den + w * l_scr[p, rows, :]
        o_ref[rows, :] = (num / den).astype(BF16)
        return carry

    step = (pl.program_id(0) * pl.num_programs(1) + pl.program_id(1)) * pl.num_programs(2) + quarter
    head_base = (step % (N_HEADS // DEC_HEAD_GROUP)) * DEC_HEAD_GROUP

    def quarter_body(groups):
        dec = _DecodeGroup(head_base, dq_ref, dkn_ref, dvn_ref, dkt_ref, dvt_ref, dbias_ref, dbiasn_ref,
                           do_ref)
        dec.scores()
        attend(groups[0])
        dec.outputs()
        for g in groups[1:]:
            attend(g)

    @pl.when(quarter == 0)
    def _():
        quarter_body([pat16(0), pat16(1), pat1(0)])

    @pl.when(quarter == 1)
    def _():
        quarter_body([pat16(2), pat16(3), pat1(1)])

    @pl.when(quarter == 2)
    def _():
        quarter_body([first4 + [unit1(0, True)], rest4[:6], rest4[6:]])

    @pl.when(quarter == 3)
    def _():
        quarter_body([pat1(2)])
        lax.fori_loop(0, SUPER // 256, merge, 0)


PROMPT_QUARTERS = 4


def _prompt_attention(q, k, v, decode, batch0, nbatch):
    s = q.shape[0]
    assert s % SUPER == 0
    bias = jnp.asarray(_prompt_bias_table())
    cur = lambda b, p, c: (b, p)
    prev = lambda b, p, c: (jnp.maximum(b - 1, 0), p)
    blk = (SUPER, LANES)
    npair = ATT_WIDTH // LANES
    grid = (s // SUPER, npair, PROMPT_QUARTERS)
    step_of = lambda b, p, c: (b * npair + p) * PROMPT_QUARTERS + c
    d_specs, d_operands, d_out_spec, d_out_shape = _decode_host_specs(
        decode, batch0, nbatch, grid[0] * grid[1] * grid[2], step_of)
    return pl.pallas_call(
        _prompt_attn_kernel,
        grid=grid,
        in_specs=[
            pl.BlockSpec(blk, cur),
            pl.BlockSpec(blk, cur),
            pl.BlockSpec(blk, prev),
            pl.BlockSpec(blk, cur),
            pl.BlockSpec(blk, prev),
            pl.BlockSpec((2, 3, 2, Q_TILE, 2 * Q_TILE), lambda b, p, c: (0, 0, p, 0, 0)),
        ] + d_specs,
        out_specs=[pl.BlockSpec(blk, cur), d_out_spec],
        out_shape=[jax.ShapeDtypeStruct((s, ATT_WIDTH), BF16), d_out_shape],
        scratch_shapes=[pltpu.VMEM((3, SUPER, LANES), F32)] * 3,
        compiler_params=pltpu.CompilerParams(
            dimension_semantics=("arbitrary", "arbitrary", "arbitrary"), vmem_limit_bytes=VMEM_LIMIT),
        name="prompt_attention_with_decode_attention",
    )(q, k, k, v, v, bias, *d_operands)


DEC_SEQ = 8
DEC_HEAD_GROUP = 8


def _decode_bias_tables(past_len):
    dist_max = MAX_WINDOW
    mult = np.zeros(dist_max + 1, np.float64)
    for w, d in zip(WINDOWS, DILATIONS):
        mult[np.arange(w // d + 1) * d] += 1.0
    slopes = _alibi_slopes()

    def table(dist):
        ok = (dist >= 0) & (dist <= dist_max)
        dc = np.clip(dist, 0, dist_max)
        mu = np.where(ok, mult[dc], 0.0)
        with np.errstate(divide="ignore"):
            logm = np.where(mu > 0, np.log(np.maximum(mu, 1e-30)), -np.inf)
        return ((-slopes[:, None, None] * dc[None].astype(np.float64) + logm[None]) * LOG2E).astype(np.float32)

    qpos = past_len + np.arange(DEC_SEQ)
    old = table(qpos[:, None] - np.arange(past_len)[None, :])
    new = np.full((N_HEADS, DEC_SEQ, 2 * DEC_SEQ), -np.inf, np.float32)
    new[:, :, :DEC_SEQ] = table(qpos[:, None] - qpos[None, :])
    return old, new


def _pad16(a):
    return jnp.concatenate([a, jnp.zeros_like(a)], axis=0).astype(BF16)


class _DecodeGroup:
    def __init__(self, head_base, q_ref, kn_ref, vn_ref, kt_ref, vt_ref, bias_ref, biasn_ref, o_ref):
        self.head_base = head_base
        self.refs = (q_ref, kn_ref, vn_ref, kt_ref, vt_ref, bias_ref, biasn_ref, o_ref)

    def scores(self):
        q_ref, kn_ref, _, kt_ref, _, bias_ref, biasn_ref, _ = self.refs
        self.s = []
        for u in range(DEC_HEAD_GROUP):
            c = slice(u * HEAD_DIM, (u + 1) * HEAD_DIM)
            q = _pad16(q_ref[:, c])
            s = _dot(q, kt_ref[u].astype(BF16))[:DEC_SEQ] + bias_ref[self.head_base + u]
            sn = _dot_nt(q, _pad16(kn_ref[:, c]))[:DEC_SEQ] + biasn_ref[self.head_base + u]
            self.s.append((s, sn))

    def outputs(self):
        _, _, vn_ref, _, vt_ref, _, _, o_ref = self.refs
        probs = []
        for s, sn in self.s:
            m = jnp.maximum(jnp.max(s, axis=1, keepdims=True), jnp.max(sn, axis=1, keepdims=True))
            probs.append((jnp.exp2(s - m), jnp.exp2(sn - m)))
        for u, (p, pn) in enumerate(probs):
            c = slice(u * HEAD_DIM, (u + 1) * HEAD_DIM)
            l = jnp.sum(p, axis=1, keepdims=True) + jnp.sum(pn, axis=1, keepdims=True)
            o = (_dot_nt(_pad16(p), vt_ref[u].astype(BF16))[:DEC_SEQ]
                 + _dot(_pad16(pn), _pad16(vn_ref[:, c]))[:DEC_SEQ])
            o_ref[:, c] = o / l


def _decode_host_specs(decode, batch0, nbatch, nsteps, step_of):
    q, kn, vn, kt, vt = decode
    nb, _, _, past = kt.shape
    gpr = N_HEADS // DEC_HEAD_GROUP
    gw = DEC_HEAD_GROUP * HEAD_DIM
    assert q.shape == (nb, DEC_SEQ, ATT_WIDTH) and nsteps == nbatch * gpr and batch0 + nbatch <= nb
    old, new = _decode_bias_tables(past)
    small = pl.BlockSpec((None, DEC_SEQ, gw),
                         lambda *g: (batch0 + step_of(*g) // gpr, 0, step_of(*g) % gpr))
    big = pl.BlockSpec((None, DEC_HEAD_GROUP, HEAD_DIM, past),
                       lambda *g: (batch0 + step_of(*g) // gpr, step_of(*g) % gpr, 0, 0))
    const3 = lambda *g: (0, 0, 0)
    in_specs = [small, small, small, big, big,
                pl.BlockSpec((N_HEADS, DEC_SEQ, past), const3),
                pl.BlockSpec((N_HEADS, DEC_SEQ, 2 * DEC_SEQ), const3)]
    out_spec = pl.BlockSpec((None, DEC_SEQ, gw), lambda *g: (step_of(*g) // gpr, 0, step_of(*g) % gpr))
    out_shape = jax.ShapeDtypeStruct((nbatch, DEC_SEQ, ATT_WIDTH), F32)
    return in_specs, (q, kn, vn, kt, vt, jnp.asarray(old), jnp.asarray(new)), out_spec, out_shape


def _outproj_kernel(x_ref, att_ref, cz_ref, wo_ref, g2_ref, x1_ref, h2_ref):
    mixed = jnp.concatenate([att_ref[...].astype(BF16), cz_ref[...]], axis=1)
    y = x_ref[...] + _dot(mixed, wo_ref[...])
    x1_ref[...] = y
    ms = jnp.mean(y * y, axis=-1, keepdims=True)
    h2_ref[...] = (y * lax.rsqrt(ms + EPS) * g2_ref[...]).astype(BF16)


def _outproj(x, att, cz, w_out, g2, *, tm=OUTPROJ_ROWS):
    m = x.shape[0]
    assert m % tm == 0
    row = lambda i: (i, 0)
    const = lambda i: (0, 0)
    return pl.pallas_call(
        _outproj_kernel,
        grid=(m // tm,),
        in_specs=[
            pl.BlockSpec((tm, D_MODEL), row),
            pl.BlockSpec((tm, ATT_WIDTH), row),
            pl.BlockSpec((tm, CONV_WIDTH), row),
            pl.BlockSpec((D_MODEL, D_MODEL), const),
            pl.BlockSpec((1, D_MODEL), const),
        ],
        out_specs=[pl.BlockSpec((tm, D_MODEL), row), pl.BlockSpec((tm, D_MODEL), row)],
        out_shape=[jax.ShapeDtypeStruct((m, D_MODEL), F32), jax.ShapeDtypeStruct((m, D_MODEL), BF16)],
        compiler_params=pltpu.CompilerParams(
            dimension_semantics=("arbitrary",), vmem_limit_bytes=VMEM_LIMIT),
        name="outproj",
    )(x, att, cz, w_out, g2)


def _ffn_kernel(x1_ref, h2_ref, wu_ref, wd_ref, *rest, groups_per_row):
    y_ref = rest[-2] if groups_per_row else rest[-1]

    @pl.when(pl.program_id(1) == 0)
    def _():
        y_ref[...] = x1_ref[...]

    def ffn_rows(rows):
        a = _dot(h2_ref[rows, :], wu_ref[...])
        g = jnp.square(jnp.maximum(a, 0.0)).astype(BF16)
        y_ref[rows, :] += _dot(g, wd_ref[...])

    if not groups_per_row:
        ffn_rows(slice(None))
        return

    q_ref, kn_ref, vn_ref, kt_ref, vt_ref, bias_ref, biasn_ref, _, o_ref = rest
    step = pl.program_id(0) * pl.num_programs(1) + pl.program_id(1)
    head_base = (step % groups_per_row) * DEC_HEAD_GROUP
    dec = _DecodeGroup(head_base, q_ref, kn_ref, vn_ref, kt_ref, vt_ref, bias_ref, biasn_ref, o_ref)
    half = h2_ref.shape[0] // 2
    dec.scores()
    ffn_rows(slice(0, half))
    dec.outputs()
    ffn_rows(slice(half, 2 * half))


def _ffn(x1, h2, w_up, w_down, decode=None, batch0=0, nbatch=0, *, tm=FFN_ROWS, tf=FFN_COLS):
    m = x1.shape[0]
    assert m % tm == 0 and D_FF % tf == 0
    nf = D_FF // tf
    row = lambda i, f: (i, 0)
    once = dict(pipeline_mode=pl.Buffered(1)) if decode is not None else {}
    in_specs = [
        pl.BlockSpec((tm, D_MODEL), row, **once),
        pl.BlockSpec((tm, D_MODEL), row, **once),
        pl.BlockSpec((D_MODEL, tf), lambda i, f: (0, f)),
        pl.BlockSpec((tf, D_MODEL), lambda i, f: (f, 0)),
    ]
    y_spec = pl.BlockSpec((tm, D_MODEL), row)
    y_shape = jax.ShapeDtypeStruct((m, D_MODEL), F32)
    if decode is None:
        return pl.pallas_call(
            functools.partial(_ffn_kernel, groups_per_row=0),
            grid=(m // tm, nf), in_specs=in_specs, out_specs=y_spec, out_shape=y_shape,
            compiler_params=pltpu.CompilerParams(
                dimension_semantics=("arbitrary", "arbitrary"), vmem_limit_bytes=VMEM_LIMIT),
            name="ffn",
        )(x1, h2, w_up, w_down)

    d_specs, d_operands, d_out_spec, d_out_shape = _decode_host_specs(
        decode, batch0, nbatch, (m // tm) * nf, lambda i, f: i * nf + f)
    return pl.pallas_call(
        functools.partial(_ffn_kernel, groups_per_row=N_HEADS // DEC_HEAD_GROUP),
        grid=(m // tm, nf),
        in_specs=in_specs + d_specs,
        out_specs=[y_spec, d_out_spec],
        out_shape=[y_shape, d_out_shape],
        compiler_params=pltpu.CompilerParams(
            dimension_semantics=("arbitrary", "arbitrary"), vmem_limit_bytes=VMEM_LIMIT_HOST),
        name="ffn_with_decode_attention",
    )(x1, h2, w_up, w_down, *d_operands)


def _head_sum_matrices():
    head_of_col = np.arange(ATT_WIDTH) // HEAD_DIM
    red = (head_of_col[:, None] == np.arange(LANES)[None, :]).astype(np.float32)
    return jnp.asarray(red, BF16), jnp.asarray(np.concatenate([red.T, red.T], axis=0), BF16)


def kernel(x_prompt, x_sample, state_k, state_v, state_conv, norm1_g, w_in, q_norm_g, k_norm_g,
           conv_w, w_out, norm2_g, w_up, w_down):
    depth = w_in.shape[0]
    assert depth == 1
    bp, sp, _ = x_prompt.shape
    nb, t, _ = x_sample.shape
    past = state_k.shape[2]
    assert bp == 1 and t == 8 and past == MAX_WINDOW and sp >= MAX_WINDOW

    red, expand = _head_sum_matrices()
    g1 = norm1_g[0][None]
    g2 = norm2_g[0][None]
    qg = jnp.tile(q_norm_g[0], N_HEADS)[None]
    kg = jnp.tile(k_norm_g[0], N_HEADS)[None]
    cw = conv_w[0]
    w_in_b = w_in[0].astype(BF16)

    xp = x_prompt[0]
    zero_conv = jnp.zeros((CONV_K - 1, CONV_WIDTH), F32)
    qp, kp, vp, czp, up, w_out_b, w_up_b, w_down_b = _inproj(
        xp, zero_conv, g1, w_in_b, qg, kg, red, expand, cw, grouped=False,
        casts=(w_out[0], w_up[0], w_down[0]))

    xs = x_sample.reshape(nb * t, D_MODEL)
    qs, ks, vs, czs, us = _inproj(xs, state_conv[0], g1, w_in_b, qg, kg, red, expand, cw, grouped=True)
    kt = jnp.transpose(state_k[0], (0, 2, 3, 1))
    vt = jnp.transpose(state_v[0], (0, 2, 3, 1))
    per_batch = lambda a: a.reshape(nb, t, ATT_WIDTH)
    decode = (per_batch(qs), per_batch(ks), per_batch(vs), kt, vt)
    gpr = N_HEADS // DEC_HEAD_GROUP
    nb_attn = (sp // SUPER) * (ATT_WIDTH // LANES) * PROMPT_QUARTERS // gpr
    assert 0 < nb_attn < nb

    attp, atts_a = _prompt_attention(qp, kp, vp, decode, 0, nb_attn)
    x1p, h2p = _outproj(xp, attp, czp, w_out_b, g2)
    yp, atts_f = _ffn(x1p, h2p, w_up_b, w_down_b, decode, nb_attn, nb - nb_attn)
    atts = jnp.concatenate([atts_a, atts_f], axis=0).reshape(nb * t, ATT_WIDTH)
    x1s, h2s = _outproj(xs, atts, czs, w_out_b, g2)
    ys = _ffn(x1s, h2s, w_up_b, w_down_b)

    keep = min(MAX_WINDOW, sp)
    new_k_prompt = kp[sp - keep:].reshape(1, 1, keep, N_HEADS, HEAD_DIM)
    new_v_prompt = vp[sp - keep:].reshape(1, 1, keep, N_HEADS, HEAD_DIM)
    new_conv_prompt = up[8 - (CONV_K - 1):].reshape(1, 1, CONV_K - 1, CONV_WIDTH)
    new_k_sample = ks.reshape(1, nb, t, N_HEADS, HEAD_DIM)
    new_v_sample = vs.reshape(1, nb, t, N_HEADS, HEAD_DIM)
    new_conv_sample = us.reshape(nb, t, CONV_WIDTH)[:, t - (CONV_K - 1):][None]
    return (yp[None], ys.reshape(nb, t, D_MODEL), new_k_prompt, new_v_prompt, new_conv_prompt,
            new_k_sample, new_v_sample, new_conv_sample)
```

```python
import functools

import numpy as np
import jax
import jax.numpy as jnp
from jax import lax
from jax.experimental import pallas as pl
from jax.experimental.pallas import tpu as pltpu

F32 = jnp.float32
BF16 = jnp.bfloat16

D_MODEL = 2048
ATT_WIDTH = 1024
CONV_WIDTH = 1024
HEAD_DIM = 64
N_HEADS = 16
WINDOWS = (128, 512, 2048)
DILATIONS = (1, 4, 16)
MAX_WINDOW = 2048
CONV_K = 3
D_FF = 4 * D_MODEL
EPS = 1e-6
LOG2E = float(np.log2(np.e))
Q_SCALE = HEAD_DIM ** -0.5 * LOG2E

LANES = 128
Q_TILE = 128
SUPER = MAX_WINDOW
VMEM_LIMIT = 56 * 1024 * 1024
VMEM_LIMIT_HOST = 60 * 1024 * 1024

INPROJ_ROWS = 512
OUTPROJ_ROWS = 512
FFN_ROWS, FFN_COLS = 1024, 512


def _dot(a, b):
    return jnp.dot(a, b, preferred_element_type=F32)


def _dot_nt(a, b):
    return lax.dot_general(a, b, (((1,), (1,)), ((), ())), preferred_element_type=F32)


def _alibi_slopes():
    return 2.0 ** (-8.0 * np.arange(1, N_HEADS + 1, dtype=np.float64) / N_HEADS)


def _head_rmsnorm(p, gain_row, red, expand):
    ss = _dot((p * p).astype(BF16), red)
    inv = lax.rsqrt(ss * (1.0 / HEAD_DIM) + EPS)
    hi = inv.astype(BF16)
    lo = (inv - hi.astype(F32)).astype(BF16)
    inv_e = _dot(jnp.concatenate([hi, lo], axis=1), expand)
    return p * inv_e * gain_row


def _inproj_kernel(*refs, tm, grouped, n_cast):
    x_ref, g1_ref, w_ref, qg_ref, kg_ref, red_ref, exp_ref, cw_ref, past_ref = refs[:9]
    cast_in = refs[9:9 + n_cast]
    q_ref, k_ref, v_ref, cz_ref, u_ref = refs[9 + n_cast:14 + n_cast]
    cast_out = refs[14 + n_cast:14 + 2 * n_cast]
    h_scr, gb_scr, c_scr, ubuf = refs[14 + 2 * n_cast:]
    i = pl.program_id(0)
    j = pl.program_id(1)

    @pl.when(j == 0)
    def _():
        x = x_ref[...]
        ms = jnp.mean(x * x, axis=-1, keepdims=True)
        h_scr[...] = (x * lax.rsqrt(ms + EPS) * g1_ref[...]).astype(BF16)

    proj = _dot(h_scr[...], w_ref[...])

    for src, dst in zip(cast_in, cast_out):
        dst[...] = src[...].astype(BF16)

    @pl.when(j == 0)
    def _():
        q_ref[...] = _head_rmsnorm(proj, qg_ref[...], red_ref[...], exp_ref[...]) * Q_SCALE

    @pl.when(j == 1)
    def _():
        k_ref[...] = _head_rmsnorm(proj, kg_ref[...], red_ref[...], exp_ref[...])

    @pl.when(j == 2)
    def _():
        v_ref[...] = proj

    @pl.when(j == 3)
    def _():
        gb_scr[...] = proj

    @pl.when(j == 4)
    def _():
        c_scr[...] = proj

    @pl.when(j == 5)
    def _():
        u = c_scr[...] * proj
        cw = cw_ref[...]
        if grouped:
            g = tm // 8
            u3 = u.reshape(g, 8, CONV_WIDTH)
            past = past_ref[...]
            p0 = past[:, 0:1, :]
            p1 = past[:, 1:2, :]
            tok = lax.broadcasted_iota(jnp.int32, u3.shape, 1)
            prev1 = jnp.where(tok == 0, p1, pltpu.roll(u3, 1, axis=1))
            prev2 = jnp.where(tok == 0, p0, jnp.where(tok == 1, p1, pltpu.roll(u3, 2, axis=1)))
            conv = cw[0:1][None] * prev2 + cw[1:2][None] * prev1 + cw[2:3][None] * u3
            cz_ref[...] = (gb_scr[...] * conv.reshape(tm, CONV_WIDTH)).astype(BF16)
            u_ref[...] = u
        else:
            @pl.when(i == 0)
            def _():
                ubuf[0:8, :] = jnp.zeros((8, CONV_WIDTH), F32)
                ubuf[6:8, :] = past_ref[...]

            ubuf[8:tm + 8, :] = u
            conv = cw[0:1] * ubuf[6:tm + 6, :] + cw[1:2] * ubuf[7:tm + 7, :] + cw[2:3] * u
            cz_ref[...] = (gb_scr[...] * conv).astype(BF16)
            tail = ubuf[tm:tm + 8, :]
            ubuf[0:8, :] = tail
            u_ref[...] = tail


CAST_SLABS = 64


def _inproj(x, past, g1, w_in, qg, kg, red, expand, cw, *, grouped, casts=(), tm=INPROJ_ROWS):
    m = x.shape[0]
    assert m % tm == 0
    nt = m // tm
    row = lambda i, j: (i, 0)
    const2 = lambda i, j: (0, 0)
    assert not casts or nt * 6 >= CAST_SLABS
    slab = lambda i, j: (jnp.minimum(i * 6 + j, CAST_SLABS - 1), 0)
    cast_specs = [pl.BlockSpec((a.shape[0] // CAST_SLABS, a.shape[1]), slab) for a in casts]
    cast_shapes = [jax.ShapeDtypeStruct(a.shape, BF16) for a in casts]
    assert all(a.shape[0] % (16 * CAST_SLABS) == 0 for a in casts)
    if grouped:
        past_spec = pl.BlockSpec((tm // 8, CONV_K - 1, CONV_WIDTH), lambda i, j: (i, 0, 0))
        u_shape = jax.ShapeDtypeStruct((m, CONV_WIDTH), F32)
        u_spec = pl.BlockSpec((tm, CONV_WIDTH), row)
    else:
        past_spec = pl.BlockSpec((CONV_K - 1, CONV_WIDTH), const2)
        u_shape = jax.ShapeDtypeStruct((8, CONV_WIDTH), F32)
        u_spec = pl.BlockSpec((8, CONV_WIDTH), const2)
    out_f32 = jax.ShapeDtypeStruct((m, ATT_WIDTH), F32)
    return pl.pallas_call(
        functools.partial(_inproj_kernel, tm=tm, grouped=grouped, n_cast=len(casts)),
        grid=(nt, 6),
        in_specs=[
            pl.BlockSpec((tm, D_MODEL), row),
            pl.BlockSpec((1, D_MODEL), const2),
            pl.BlockSpec((D_MODEL, 1024), lambda i, j: (0, j)),
            pl.BlockSpec((1, ATT_WIDTH), const2),
            pl.BlockSpec((1, ATT_WIDTH), const2),
            pl.BlockSpec((ATT_WIDTH, LANES), const2),
            pl.BlockSpec((2 * LANES, ATT_WIDTH), const2),
            pl.BlockSpec((CONV_K, CONV_WIDTH), const2),
            past_spec,
        ] + cast_specs,
        out_specs=[
            pl.BlockSpec((tm, ATT_WIDTH), row),
            pl.BlockSpec((tm, ATT_WIDTH), row),
            pl.BlockSpec((tm, ATT_WIDTH), row),
            pl.BlockSpec((tm, CONV_WIDTH), row),
            u_spec,
        ] + cast_specs,
        out_shape=[out_f32, out_f32, out_f32,
                   jax.ShapeDtypeStruct((m, CONV_WIDTH), BF16), u_shape] + cast_shapes,
        scratch_shapes=[
            pltpu.VMEM((tm, D_MODEL), BF16),
            pltpu.VMEM((tm, CONV_WIDTH), F32),
            pltpu.VMEM((tm, CONV_WIDTH), F32),
            pltpu.VMEM((tm + 8, CONV_WIDTH), F32),
        ],
        compiler_params=pltpu.CompilerParams(
            dimension_semantics=("arbitrary", "arbitrary"), vmem_limit_bytes=VMEM_LIMIT),
        name="inproj_grouped" if grouped else "inproj_seq",
    )(x, g1, w_in, qg, kg, red, expand, cw, past, *casts)


def _prompt_bias_table():
    qi = np.arange(Q_TILE)[:, None]
    kc = np.arange(2 * Q_TILE)[None, :]
    jdist = (Q_TILE + qi - kc).astype(np.float64)
    valid = (jdist >= 0) & (jdist <= Q_TILE)
    slopes = _alibi_slopes()
    tab = np.empty((2, len(DILATIONS), N_HEADS, Q_TILE, 2 * Q_TILE), np.float32)
    for f in range(2):
        ok = valid & (kc >= Q_TILE) if f else valid
        for p, d in enumerate(DILATIONS):
            for h in range(N_HEADS):
                tab[f, p, h] = np.where(ok, -slopes[h] * d * jdist * LOG2E, -np.inf)
    return tab


def _prompt_attn_kernel(q_ref, kc_ref, vc_ref, bias_ref,
                        dq_ref, dkn_ref, dvn_ref, dkt_ref, dvt_ref, dbias_ref, dbiasn_ref,
                        o_ref, do_ref, o_scr, m_scr, l_scr, kp_ref, vp_ref):
    first = jnp.where(pl.program_id(1) == 0, 1, 0)
    quarter = pl.program_id(2)

    @pl.when((pl.program_id(1) == 0) & (quarter == 0))
    def _():
        kp_ref[...] = jnp.zeros(kp_ref.shape, F32)
        vp_ref[...] = jnp.zeros(vp_ref.shape, F32)
    lane = lax.broadcasted_iota(jnp.int32, (Q_TILE, LANES), 1)
    lo = lane < HEAD_DIM
    hi = jnp.logical_not(lo)

    def attend(units):
        def scores(unit):
            pat, variant, rows, kprev_ref, vprev_ref, prev = unit
            q = q_ref[rows, :]
            k2 = jnp.concatenate([kprev_ref[prev, :], kc_ref[rows, :]], axis=0).astype(BF16)
            q2 = jnp.concatenate([jnp.where(lo, q, 0.0), jnp.where(hi, q, 0.0)], axis=0).astype(BF16)
            return _dot_nt(q2, k2) + bias_ref[variant, pat].reshape(2 * Q_TILE, 2 * Q_TILE)

        def softmax(s):
            m = jnp.max(s, axis=1, keepdims=True)
            return m, jnp.exp2(s - m).astype(BF16)

        def output(unit, stats):
            pat, _, rows, _, vprev_ref, prev = unit
            v2 = jnp.concatenate([vprev_ref[prev, :], vc_ref[rows, :]], axis=0).astype(BF16)
            v_ext = jnp.concatenate([v2, jnp.ones_like(v2)], axis=1)
            m, p = stats
            f = _dot(p, v_ext)
            o_scr[pat, rows, :] = jnp.where(lo, f[:Q_TILE, :LANES], f[Q_TILE:, :LANES])
            m_scr[pat, rows, :] = jnp.where(lo, m[:Q_TILE], m[Q_TILE:])
            l_scr[pat, rows, :] = jnp.where(lo, f[:Q_TILE, LANES:], f[Q_TILE:, LANES:])

        all_scores = [scores(unit) for unit in units]
        all_stats = [softmax(ss) for ss in all_scores]
        for unit, stats in zip(units, all_stats):
            output(unit, stats)

    def unit16(r):
        rows = pl.ds(r, Q_TILE, stride=16)
        return (2, first, rows, kp_ref, vp_ref, rows)

    def unit4(start, from_prev_block):
        rows = pl.ds(start, Q_TILE, stride=4)
        if from_prev_block:
            return (1, first, rows, kp_ref, vp_ref, pl.ds(SUPER - 512 + start, Q_TILE, stride=4))
        return (1, 0, rows, kc_ref, vc_ref, pl.ds(start - 512, Q_TILE, stride=4))

    def unit1(start, from_prev_block):
        rows = pl.ds(start, Q_TILE)
        if from_prev_block:
            return (0, first, rows, kp_ref, vp_ref, pl.ds(SUPER - Q_TILE, Q_TILE))
        return (0, 0, rows, kc_ref, vc_ref, pl.ds(start - Q_TILE, Q_TILE))

    def pat16(g):
        return [unit16(g * 4 + u) for u in range(4)]

    first4 = [unit4(r4, True) for r4 in range(4)]
    rest4 = [unit4(512 * c + r4, False) for c in range(1, SUPER // 512) for r4 in range(4)]

    def pat1(g):
        return [unit1((1 + 5 * g + u) * Q_TILE, False) for u in range(5)]

    def merge(c, carry):
        rows = pl.ds(pl.multiple_of(c * 256, 256), 256)
        ms = [m_scr[p, rows, :] for p in range(3)]
        mx = jnp.maximum(jnp.maximum(ms[0], ms[1]), ms[2])
        num = jnp.zeros((256, LANES), F32)
        den = jnp.zeros((256, LANES), F32)
        for p in range(3):
            w = jnp.exp2(ms[p] - mx)
            num = num + w * o_scr[p, rows, :]
            den = den + w * l_scr[p, rows, :]
        o_ref[rows, :] = (num / den).astype(BF16)
        return carry

    step = (pl.program_id(0) * pl.num_programs(1) + pl.program_id(1)) * pl.num_programs(2) + quarter
    head_base = (step % (N_HEADS // DEC_HEAD_GROUP)) * DEC_HEAD_GROUP

    def quarter_body(groups):
        dec = _DecodeGroup(head_base, dq_ref, dkn_ref, dvn_ref, dkt_ref, dvt_ref, dbias_ref, dbiasn_ref,
                           do_ref)
        dec.scores()
        attend(groups[0])
        dec.outputs()
        for g in groups[1:]:
            attend(g)

    @pl.when(quarter == 0)
    def _():
        quarter_body([pat16(0), pat16(1), pat1(0)])

    @pl.when(quarter == 1)
    def _():
        quarter_body([pat16(2), pat16(3), pat1(1)])

    @pl.when(quarter == 2)
    def _():
        quarter_body([first4 + [unit1(0, True)], rest4[:6], rest4[6:]])

    @pl.when(quarter == 3)
    def _():
        quarter_body([pat1(2)])
        lax.fori_loop(0, SUPER // 256, merge, 0)
        kp_ref[...] = kc_ref[...]
        vp_ref[...] = vc_ref[...]


PROMPT_QUARTERS = 4


def _prompt_attention(q, k, v, decode, batch0, nbatch):
    s = q.shape[0]
    assert s % SUPER == 0
    bias = jnp.asarray(_prompt_bias_table())
    cur = lambda p, b, c: (b, p)
    blk = (SUPER, LANES)
    nblk = s // SUPER
    grid = (ATT_WIDTH // LANES, nblk, PROMPT_QUARTERS)
    step_of = lambda p, b, c: (p * nblk + b) * PROMPT_QUARTERS + c
    d_specs, d_operands, d_out_spec, d_out_shape = _decode_host_specs(
        decode, batch0, nbatch, grid[0] * grid[1] * grid[2], step_of)
    return pl.pallas_call(
        _prompt_attn_kernel,
        grid=grid,
        in_specs=[
            pl.BlockSpec(blk, cur),
            pl.BlockSpec(blk, cur),
            pl.BlockSpec(blk, cur),
            pl.BlockSpec((2, 3, 2, Q_TILE, 2 * Q_TILE), lambda p, b, c: (0, 0, p, 0, 0)),
        ] + d_specs,
        out_specs=[pl.BlockSpec(blk, cur), d_out_spec],
        out_shape=[jax.ShapeDtypeStruct((s, ATT_WIDTH), BF16), d_out_shape],
        scratch_shapes=[pltpu.VMEM((3, SUPER, LANES), F32)] * 3 + [pltpu.VMEM(blk, F32)] * 2,
        compiler_params=pltpu.CompilerParams(
            dimension_semantics=("arbitrary", "arbitrary", "arbitrary"), vmem_limit_bytes=VMEM_LIMIT),
        name="prompt_attention_with_decode_attention",
    )(q, k, v, bias, *d_operands)


DEC_SEQ = 8
DEC_HEAD_GROUP = 8


def _decode_bias_tables(past_len):
    dist_max = MAX_WINDOW
    mult = np.zeros(dist_max + 1, np.float64)
    for w, d in zip(WINDOWS, DILATIONS):
        mult[np.arange(w // d + 1) * d] += 1.0
    slopes = _alibi_slopes()

    def table(dist):
        ok = (dist >= 0) & (dist <= dist_max)
        dc = np.clip(dist, 0, dist_max)
        mu = np.where(ok, mult[dc], 0.0)
        with np.errstate(divide="ignore"):
            logm = np.where(mu > 0, np.log(np.maximum(mu, 1e-30)), -np.inf)
        return ((-slopes[:, None, None] * dc[None].astype(np.float64) + logm[None]) * LOG2E).astype(np.float32)

    qpos = past_len + np.arange(DEC_SEQ)
    old = table(qpos[:, None] - np.arange(past_len)[None, :])
    new = np.full((N_HEADS, DEC_SEQ, 2 * DEC_SEQ), -np.inf, np.float32)
    new[:, :, :DEC_SEQ] = table(qpos[:, None] - qpos[None, :])
    return old, new


def _pad16(a):
    return jnp.concatenate([a, jnp.zeros_like(a)], axis=0).astype(BF16)


class _DecodeGroup:
    def __init__(self, head_base, q_ref, kn_ref, vn_ref, kt_ref, vt_ref, bias_ref, biasn_ref, o_ref):
        self.head_base = head_base
        self.refs = (q_ref, kn_ref, vn_ref, kt_ref, vt_ref, bias_ref, biasn_ref, o_ref)

    def scores(self):
        q_ref, kn_ref, _, kt_ref, _, bias_ref, biasn_ref, _ = self.refs
        self.s = []
        for u in range(DEC_HEAD_GROUP):
            c = slice(u * HEAD_DIM, (u + 1) * HEAD_DIM)
            q = _pad16(q_ref[:, c])
            s = _dot(q, kt_ref[u].astype(BF16))[:DEC_SEQ] + bias_ref[self.head_base + u]
            sn = _dot_nt(q, _pad16(kn_ref[:, c]))[:DEC_SEQ] + biasn_ref[self.head_base + u]
            self.s.append((s, sn))

    def outputs(self):
        _, _, vn_ref, _, vt_ref, _, _, o_ref = self.refs
        probs = []
        for s, sn in self.s:
            m = jnp.maximum(jnp.max(s, axis=1, keepdims=True), jnp.max(sn, axis=1, keepdims=True))
            probs.append((jnp.exp2(s - m), jnp.exp2(sn - m)))
        for u, (p, pn) in enumerate(probs):
            c = slice(u * HEAD_DIM, (u + 1) * HEAD_DIM)
            l = jnp.sum(p, axis=1, keepdims=True) + jnp.sum(pn, axis=1, keepdims=True)
            o = (_dot_nt(_pad16(p), vt_ref[u].astype(BF16))[:DEC_SEQ]
                 + _dot(_pad16(pn), _pad16(vn_ref[:, c]))[:DEC_SEQ])
            o_ref[:, c] = o / l


def _decode_host_specs(decode, batch0, nbatch, nsteps, step_of):
    q, kn, vn, kt, vt = decode
    nb, _, _, past = kt.shape
    gpr = N_HEADS // DEC_HEAD_GROUP
    gw = DEC_HEAD_GROUP * HEAD_DIM
    assert q.shape == (nb, DEC_SEQ, ATT_WIDTH) and nsteps == nbatch * gpr and batch0 + nbatch <= nb
    old, new = _decode_bias_tables(past)
    small = pl.BlockSpec((None, DEC_SEQ, gw),
                         lambda *g: (batch0 + step_of(*g) // gpr, 0, step_of(*g) % gpr))
    big = pl.BlockSpec((None, DEC_HEAD_GROUP, HEAD_DIM, past),
                       lambda *g: (batch0 + step_of(*g) // gpr, step_of(*g) % gpr, 0, 0))
    const3 = lambda *g: (0, 0, 0)
    in_specs = [small, small, small, big, big,
                pl.BlockSpec((N_HEADS, DEC_SEQ, past), const3),
                pl.BlockSpec((N_HEADS, DEC_SEQ, 2 * DEC_SEQ), const3)]
    out_spec = pl.BlockSpec((None, DEC_SEQ, gw), lambda *g: (step_of(*g) // gpr, 0, step_of(*g) % gpr))
    out_shape = jax.ShapeDtypeStruct((nbatch, DEC_SEQ, ATT_WIDTH), F32)
    return in_specs, (q, kn, vn, kt, vt, jnp.asarray(old), jnp.asarray(new)), out_spec, out_shape


def _outproj_kernel(x_ref, att_ref, cz_ref, wo_ref, g2_ref, x1_ref, h2_ref):
    mixed = jnp.concatenate([att_ref[...].astype(BF16), cz_ref[...]], axis=1)
    y = x_ref[...] + _dot(mixed, wo_ref[...])
    x1_ref[...] = y
    ms = jnp.mean(y * y, axis=-1, keepdims=True)
    h2_ref[...] = (y * lax.rsqrt(ms + EPS) * g2_ref[...]).astype(BF16)


def _outproj(x, att, cz, w_out, g2, *, tm=OUTPROJ_ROWS):
    m = x.shape[0]
    assert m % tm == 0
    row = lambda i: (i, 0)
    const = lambda i: (0, 0)
    return pl.pallas_call(
        _outproj_kernel,
        grid=(m // tm,),
        in_specs=[
            pl.BlockSpec((tm, D_MODEL), row),
            pl.BlockSpec((tm, ATT_WIDTH), row),
            pl.BlockSpec((tm, CONV_WIDTH), row),
            pl.BlockSpec((D_MODEL, D_MODEL), const),
            pl.BlockSpec((1, D_MODEL), const),
        ],
        out_specs=[pl.BlockSpec((tm, D_MODEL), row), pl.BlockSpec((tm, D_MODEL), row)],
        out_shape=[jax.ShapeDtypeStruct((m, D_MODEL), F32), jax.ShapeDtypeStruct((m, D_MODEL), BF16)],
        compiler_params=pltpu.CompilerParams(
            dimension_semantics=("arbitrary",), vmem_limit_bytes=VMEM_LIMIT),
        name="outproj",
    )(x, att, cz, w_out, g2)


def _ffn_kernel(x1_ref, h2_ref, wu_ref, wd_ref, *rest, groups_per_row):
    y_ref = rest[-2] if groups_per_row else rest[-1]

    @pl.when(pl.program_id(1) == 0)
    def _():
        y_ref[...] = x1_ref[...]

    def ffn_rows(rows):
        a = _dot(h2_ref[rows, :], wu_ref[...])
        g = jnp.square(jnp.maximum(a, 0.0)).astype(BF16)
        y_ref[rows, :] += _dot(g, wd_ref[...])

    if not groups_per_row:
        ffn_rows(slice(None))
        return

    q_ref, kn_ref, vn_ref, kt_ref, vt_ref, bias_ref, biasn_ref, _, o_ref = rest
    step = pl.program_id(0) * pl.num_programs(1) + pl.program_id(1)
    head_base = (step % groups_per_row) * DEC_HEAD_GROUP
    dec = _DecodeGroup(head_base, q_ref, kn_ref, vn_ref, kt_ref, vt_ref, bias_ref, biasn_ref, o_ref)
    half = h2_ref.shape[0] // 2
    dec.scores()
    ffn_rows(slice(0, half))
    dec.outputs()
    ffn_rows(slice(half, 2 * half))


def _ffn(x1, h2, w_up, w_down, decode=None, batch0=0, nbatch=0, *, tm=FFN_ROWS, tf=FFN_COLS):
    m = x1.shape[0]
    assert m % tm == 0 and D_FF % tf == 0
    nf = D_FF // tf
    row = lambda i, f: (i, 0)
    once = dict(pipeline_mode=pl.Buffered(1)) if decode is not None else {}
    in_specs = [
        pl.BlockSpec((tm, D_MODEL), row, **once),
        pl.BlockSpec((tm, D_MODEL), row, **once),
        pl.BlockSpec((D_MODEL, tf), lambda i, f: (0, f)),
        pl.BlockSpec((tf, D_MODEL), lambda i, f: (f, 0)),
    ]
    y_spec = pl.BlockSpec((tm, D_MODEL), row)
    y_shape = jax.ShapeDtypeStruct((m, D_MODEL), F32)
    if decode is None:
        return pl.pallas_call(
            functools.partial(_ffn_kernel, groups_per_row=0),
            grid=(m // tm, nf), in_specs=in_specs, out_specs=y_spec, out_shape=y_shape,
            compiler_params=pltpu.CompilerParams(
                dimension_semantics=("arbitrary", "arbitrary"), vmem_limit_bytes=VMEM_LIMIT),
            name="ffn",
        )(x1, h2, w_up, w_down)

    d_specs, d_operands, d_out_spec, d_out_shape = _decode_host_specs(
        decode, batch0, nbatch, (m // tm) * nf, lambda i, f: i * nf + f)
    return pl.pallas_call(
        functools.partial(_ffn_kernel, groups_per_row=N_HEADS // DEC_HEAD_GROUP),
        grid=(m // tm, nf),
        in_specs=in_specs + d_specs,
        out_specs=[y_spec, d_out_spec],
        out_shape=[y_shape, d_out_shape],
        compiler_params=pltpu.CompilerParams(
            dimension_semantics=("arbitrary", "arbitrary"), vmem_limit_bytes=VMEM_LIMIT_HOST),
        name="ffn_with_decode_attention",
    )(x1, h2, w_up, w_down, *d_operands)


def _head_sum_matrices():
    head_of_col = np.arange(ATT_WIDTH) // HEAD_DIM
    red = (head_of_col[:, None] == np.arange(LANES)[None, :]).astype(np.float32)
    return jnp.asarray(red, BF16), jnp.asarray(np.concatenate([red.T, red.T], axis=0), BF16)


def kernel(x_prompt, x_sample, state_k, state_v, state_conv, norm1_g, w_in, q_norm_g, k_norm_g,
           conv_w, w_out, norm2_g, w_up, w_down):
    depth = w_in.shape[0]
    assert depth == 1
    bp, sp, _ = x_prompt.shape
    nb, t, _ = x_sample.shape
    past = state_k.shape[2]
    assert bp == 1 and t == 8 and past == MAX_WINDOW and sp >= MAX_WINDOW

    red, expand = _head_sum_matrices()
    g1 = norm1_g[0][None]
    g2 = norm2_g[0][None]
    qg = jnp.tile(q_norm_g[0], N_HEADS)[None]
    kg = jnp.tile(k_norm_g[0], N_HEADS)[None]
    cw = conv_w[0]
    w_in_b = w_in[0].astype(BF16)

    xp = x_prompt[0]
    zero_conv = jnp.zeros((CONV_K - 1, CONV_WIDTH), F32)
    qp, kp, vp, czp, up, w_out_b, w_up_b, w_down_b = _inproj(
        xp, zero_conv, g1, w_in_b, qg, kg, red, expand, cw, grouped=False,
        casts=(w_out[0], w_up[0], w_down[0]))

    xs = x_sample.reshape(nb * t, D_MODEL)
    qs, ks, vs, czs, us = _inproj(xs, state_conv[0], g1, w_in_b, qg, kg, red, expand, cw, grouped=True)
    kt = jnp.transpose(state_k[0], (0, 2, 3, 1))
    vt = jnp.transpose(state_v[0], (0, 2, 3, 1))
    per_batch = lambda a: a.reshape(nb, t, ATT_WIDTH)
    decode = (per_batch(qs), per_batch(ks), per_batch(vs), kt, vt)
    gpr = N_HEADS // DEC_HEAD_GROUP
    nb_attn = (sp // SUPER) * (ATT_WIDTH // LANES) * PROMPT_QUARTERS // gpr
    assert 0 < nb_attn < nb

    attp, atts_a = _prompt_attention(qp, kp, vp, decode, 0, nb_attn)
    x1p, h2p = _outproj(xp, attp, czp, w_out_b, g2)
    yp, atts_f = _ffn(x1p, h2p, w_up_b, w_down_b, decode, nb_attn, nb - nb_attn)
    atts = jnp.concatenate([atts_a, atts_f], axis=0).reshape(nb * t, ATT_WIDTH)
    x1s, h2s = _outproj(xs, atts, czs, w_out_b, g2)
    ys = _ffn(x1s, h2s, w_up_b, w_down_b)

    keep = min(MAX_WINDOW, sp)
    new_k_prompt = kp[sp - keep:].reshape(1, 1, keep, N_HEADS, HEAD_DIM)
    new_v_prompt = vp[sp - keep:].reshape(1, 1, keep, N_HEADS, HEAD_DIM)
    new_conv_prompt = up[8 - (CONV_K - 1):].reshape(1, 1, CONV_K - 1, CONV_WIDTH)
    new_k_sample = ks.reshape(1, nb, t, N_HEADS, HEAD_DIM)
    new_v_sample = vs.reshape(1, nb, t, N_HEADS, HEAD_DIM)
    new_conv_sample = us.reshape(nb, t, CONV_WIDTH)[:, t - (CONV_K - 1):][None]
    return (yp[None], ys.reshape(nb, t, D_MODEL), new_k_prompt, new_v_prompt, new_conv_prompt,
            new_k_sample, new_v_sample, new_conv_sample)
```

```python
import functools

import numpy as np
import jax
import jax.numpy as jnp
from jax import lax
from jax.experimental import pallas as pl
from jax.experimental.pallas import tpu as pltpu

F32 = jnp.float32
BF16 = jnp.bfloat16

D_MODEL = 2048
ATT_WIDTH = 1024
CONV_WIDTH = 1024
HEAD_DIM = 64
N_HEADS = 16
WINDOWS = (128, 512, 2048)
DILATIONS = (1, 4, 16)
MAX_WINDOW = 2048
CONV_K = 3
D_FF = 4 * D_MODEL
EPS = 1e-6
LOG2E = float(np.log2(np.e))
Q_SCALE = HEAD_DIM ** -0.5 * LOG2E

LANES = 128
Q_TILE = 128
SUPER = MAX_WINDOW
VMEM_LIMIT = 56 * 1024 * 1024
VMEM_LIMIT_HOST = 60 * 1024 * 1024

INPROJ_ROWS = 512
OUTPROJ_ROWS = 512
FFN_ROWS, FFN_COLS = 1024, 512


def _dot(a, b):
    return jnp.dot(a, b, preferred_element_type=F32)


def _dot_nt(a, b):
    return lax.dot_general(a, b, (((1,), (1,)), ((), ())), preferred_element_type=F32)


def _alibi_slopes():
    return 2.0 ** (-8.0 * np.arange(1, N_HEADS + 1, dtype=np.float64) / N_HEADS)


def _head_rmsnorm(p, gain_row, red, expand):
    ss = _dot((p * p).astype(BF16), red)
    inv = lax.rsqrt(ss * (1.0 / HEAD_DIM) + EPS)
    hi = inv.astype(BF16)
    lo = (inv - hi.astype(F32)).astype(BF16)
    inv_e = _dot(jnp.concatenate([hi, lo], axis=1), expand)
    return p * inv_e * gain_row


def _inproj_kernel(*refs, tm, grouped, n_cast, n_tail):
    x_ref, g1_ref, w_ref, qg_ref, kg_ref, red_ref, exp_ref, cw_ref, past_ref = refs[:9]
    cast_in = refs[9:9 + n_cast]
    q_ref, k_ref, v_ref, cz_ref, u_ref = refs[9 + n_cast:14 + n_cast]
    cast_out = refs[14 + n_cast:14 + 2 * n_cast]
    tails = refs[14 + 2 * n_cast:14 + 2 * n_cast + n_tail]
    h_scr, gb_scr, c_scr, ubuf = refs[14 + 2 * n_cast + n_tail:]
    i = pl.program_id(0)
    j = pl.program_id(1)

    @pl.when(j == 0)
    def _():
        x = x_ref[...]
        ms = jnp.mean(x * x, axis=-1, keepdims=True)
        h_scr[...] = (x * lax.rsqrt(ms + EPS) * g1_ref[...]).astype(BF16)

    proj = _dot(h_scr[...], w_ref[...])

    for src, dst in zip(cast_in, cast_out):
        dst[...] = src[...].astype(BF16)

    @pl.when(j == 0)
    def _():
        q_ref[...] = _head_rmsnorm(proj, qg_ref[...], red_ref[...], exp_ref[...]) * Q_SCALE

    @pl.when(j == 1)
    def _():
        k = _head_rmsnorm(proj, kg_ref[...], red_ref[...], exp_ref[...])
        k_ref[...] = k
        if tails:
            tails[0][...] = k

    @pl.when(j == 2)
    def _():
        v_ref[...] = proj
        if tails:
            tails[1][...] = proj

    @pl.when(j == 3)
    def _():
        gb_scr[...] = proj

    @pl.when(j == 4)
    def _():
        c_scr[...] = proj

    @pl.when(j == 5)
    def _():
        u = c_scr[...] * proj
        cw = cw_ref[...]
        if grouped:
            g = tm // 8
            u3 = u.reshape(g, 8, CONV_WIDTH)
            past = past_ref[...]
            p0 = past[:, 0:1, :]
            p1 = past[:, 1:2, :]
            tok = lax.broadcasted_iota(jnp.int32, u3.shape, 1)
            prev1 = jnp.where(tok == 0, p1, pltpu.roll(u3, 1, axis=1))
            prev2 = jnp.where(tok == 0, p0, jnp.where(tok == 1, p1, pltpu.roll(u3, 2, axis=1)))
            conv = cw[0:1][None] * prev2 + cw[1:2][None] * prev1 + cw[2:3][None] * u3
            cz_ref[...] = (gb_scr[...] * conv.reshape(tm, CONV_WIDTH)).astype(BF16)
            u_ref[...] = u
        else:
            @pl.when(i == 0)
            def _():
                ubuf[0:8, :] = jnp.zeros((8, CONV_WIDTH), F32)
                ubuf[6:8, :] = past_ref[...]

            ubuf[8:tm + 8, :] = u
            conv = cw[0:1] * ubuf[6:tm + 6, :] + cw[1:2] * ubuf[7:tm + 7, :] + cw[2:3] * u
            cz_ref[...] = (gb_scr[...] * conv).astype(BF16)
            tail = ubuf[tm:tm + 8, :]
            ubuf[0:8, :] = tail
            u_ref[...] = tail


CAST_SLABS = 64


def _inproj(x, past, g1, w_in, qg, kg, red, expand, cw, *, grouped, casts=(), tail_rows=0,
            tm=INPROJ_ROWS):
    m = x.shape[0]
    assert m % tm == 0 and tail_rows % tm == 0
    nt = m // tm
    row = lambda i, j: (i, 0)
    const2 = lambda i, j: (0, 0)
    tail_map = lambda i, j: (jnp.maximum(i - (nt - tail_rows // tm), 0), 0)
    tail_specs = [pl.BlockSpec((tm, ATT_WIDTH), tail_map)] * 2 if tail_rows else []
    tail_shapes = [jax.ShapeDtypeStruct((tail_rows, ATT_WIDTH), F32)] * 2 if tail_rows else []
    assert not casts or nt * 6 >= CAST_SLABS
    slab = lambda i, j: (jnp.minimum(i * 6 + j, CAST_SLABS - 1), 0)
    cast_specs = [pl.BlockSpec((a.shape[0] // CAST_SLABS, a.shape[1]), slab) for a in casts]
    cast_shapes = [jax.ShapeDtypeStruct(a.shape, BF16) for a in casts]
    assert all(a.shape[0] % (16 * CAST_SLABS) == 0 for a in casts)
    if grouped:
        past_spec = pl.BlockSpec((tm // 8, CONV_K - 1, CONV_WIDTH), lambda i, j: (i, 0, 0))
        u_shape = jax.ShapeDtypeStruct((m, CONV_WIDTH), F32)
        u_spec = pl.BlockSpec((tm, CONV_WIDTH), row)
    else:
        past_spec = pl.BlockSpec((CONV_K - 1, CONV_WIDTH), const2)
        u_shape = jax.ShapeDtypeStruct((8, CONV_WIDTH), F32)
        u_spec = pl.BlockSpec((8, CONV_WIDTH), const2)
    out_f32 = jax.ShapeDtypeStruct((m, ATT_WIDTH), F32)
    return pl.pallas_call(
        functools.partial(_inproj_kernel, tm=tm, grouped=grouped, n_cast=len(casts),
                          n_tail=len(tail_specs)),
        grid=(nt, 6),
        in_specs=[
            pl.BlockSpec((tm, D_MODEL), row),
            pl.BlockSpec((1, D_MODEL), const2),
            pl.BlockSpec((D_MODEL, 1024), lambda i, j: (0, j)),
            pl.BlockSpec((1, ATT_WIDTH), const2),
            pl.BlockSpec((1, ATT_WIDTH), const2),
            pl.BlockSpec((ATT_WIDTH, LANES), const2),
            pl.BlockSpec((2 * LANES, ATT_WIDTH), const2),
            pl.BlockSpec((CONV_K, CONV_WIDTH), const2),
            past_spec,
        ] + cast_specs,
        out_specs=[
            pl.BlockSpec((tm, ATT_WIDTH), row),
            pl.BlockSpec((tm, ATT_WIDTH), row),
            pl.BlockSpec((tm, ATT_WIDTH), row),
            pl.BlockSpec((tm, CONV_WIDTH), row),
            u_spec,
        ] + cast_specs + tail_specs,
        out_shape=[out_f32, out_f32, out_f32,
                   jax.ShapeDtypeStruct((m, CONV_WIDTH), BF16), u_shape] + cast_shapes + tail_shapes,
        scratch_shapes=[
            pltpu.VMEM((tm, D_MODEL), BF16),
            pltpu.VMEM((tm, CONV_WIDTH), F32),
            pltpu.VMEM((tm, CONV_WIDTH), F32),
            pltpu.VMEM((tm + 8, CONV_WIDTH), F32),
        ],
        compiler_params=pltpu.CompilerParams(
            dimension_semantics=("arbitrary", "arbitrary"),
            vmem_limit_bytes=VMEM_LIMIT_HOST if tail_rows else VMEM_LIMIT),
        name="inproj_grouped" if grouped else "inproj_seq",
    )(x, g1, w_in, qg, kg, red, expand, cw, past, *casts)


def _prompt_bias_table():
    qi = np.arange(Q_TILE)[:, None]
    kc = np.arange(2 * Q_TILE)[None, :]
    jdist = (Q_TILE + qi - kc).astype(np.float64)
    valid = (jdist >= 0) & (jdist <= Q_TILE)
    slopes = _alibi_slopes()
    tab = np.empty((2, len(DILATIONS), N_HEADS, Q_TILE, 2 * Q_TILE), np.float32)
    for f in range(2):
        ok = valid & (kc >= Q_TILE) if f else valid
        for p, d in enumerate(DILATIONS):
            for h in range(N_HEADS):
                tab[f, p, h] = np.where(ok, -slopes[h] * d * jdist * LOG2E, -np.inf)
    return tab


def _prompt_attn_kernel(q_ref, kc_ref, vc_ref, bias_ref,
                        dq_ref, dkn_ref, dvn_ref, dkt_ref, dvt_ref, dbias_ref, dbiasn_ref,
                        o_ref, do_ref, o_scr, m_scr, l_scr, kp_ref, vp_ref):
    first = jnp.where(pl.program_id(1) == 0, 1, 0)
    quarter = pl.program_id(2)

    @pl.when((pl.program_id(1) == 0) & (quarter == 0))
    def _():
        kp_ref[...] = jnp.zeros(kp_ref.shape, F32)
        vp_ref[...] = jnp.zeros(vp_ref.shape, F32)
    lane = lax.broadcasted_iota(jnp.int32, (Q_TILE, LANES), 1)
    lo = lane < HEAD_DIM
    hi = jnp.logical_not(lo)

    def attend(units):
        def scores(unit):
            pat, variant, rows, kprev_ref, vprev_ref, prev = unit
            q = q_ref[rows, :]
            k2 = jnp.concatenate([kprev_ref[prev, :], kc_ref[rows, :]], axis=0).astype(BF16)
            q2 = jnp.concatenate([jnp.where(lo, q, 0.0), jnp.where(hi, q, 0.0)], axis=0).astype(BF16)
            return _dot_nt(q2, k2) + bias_ref[variant, pat].reshape(2 * Q_TILE, 2 * Q_TILE)

        def softmax(s):
            m = jnp.max(s, axis=1, keepdims=True)
            return m, jnp.exp2(s - m).astype(BF16)

        def output(unit, stats):
            pat, _, rows, _, vprev_ref, prev = unit
            v2 = jnp.concatenate([vprev_ref[prev, :], vc_ref[rows, :]], axis=0).astype(BF16)
            v_ext = jnp.concatenate([v2, jnp.ones_like(v2)], axis=1)
            m, p = stats
            f = _dot(p, v_ext)
            o_scr[pat, rows, :] = jnp.where(lo, f[:Q_TILE, :LANES], f[Q_TILE:, :LANES])
            m_scr[pat, rows, :] = jnp.where(lo, m[:Q_TILE], m[Q_TILE:])
            l_scr[pat, rows, :] = jnp.where(lo, f[:Q_TILE, LANES:], f[Q_TILE:, LANES:])

        all_scores = [scores(unit) for unit in units]
        all_stats = [softmax(ss) for ss in all_scores]
        for unit, stats in zip(units, all_stats):
            output(unit, stats)

    def unit16(r):
        rows = pl.ds(r, Q_TILE, stride=16)
        return (2, first, rows, kp_ref, vp_ref, rows)

    def unit4(start, from_prev_block):
        rows = pl.ds(start, Q_TILE, stride=4)
        if from_prev_block:
            return (1, first, rows, kp_ref, vp_ref, pl.ds(SUPER - 512 + start, Q_TILE, stride=4))
        return (1, 0, rows, kc_ref, vc_ref, pl.ds(start - 512, Q_TILE, stride=4))

    def unit1(start, from_prev_block):
        rows = pl.ds(start, Q_TILE)
        if from_prev_block:
            return (0, first, rows, kp_ref, vp_ref, pl.ds(SUPER - Q_TILE, Q_TILE))
        return (0, 0, rows, kc_ref, vc_ref, pl.ds(start - Q_TILE, Q_TILE))

    def pat16(g):
        return [unit16(g * 4 + u) for u in range(4)]

    first4 = [unit4(r4, True) for r4 in range(4)]
    rest4 = [unit4(512 * c + r4, False) for c in range(1, SUPER // 512) for r4 in range(4)]

    def pat1(g):
        return [unit1((1 + 5 * g + u) * Q_TILE, False) for u in range(5)]

    def merge(c, carry):
        rows = pl.ds(pl.multiple_of(c * 256, 256), 256)
        ms = [m_scr[p, rows, :] for p in range(3)]
        mx = jnp.maximum(jnp.maximum(ms[0], ms[1]), ms[2])
        num = jnp.zeros((256, LANES), F32)
        den = jnp.zeros((256, LANES), F32)
        for p in range(3):
            w = jnp.exp2(ms[p] - mx)
            num = num + w * o_scr[p, rows, :]
            den = den + w * l_scr[p, rows, :]
        o_ref[rows, :] = (num / den).astype(BF16)
        return carry

    step = (pl.program_id(0) * pl.num_programs(1) + pl.program_id(1)) * pl.num_programs(2) + quarter
    head_base = (step % (N_HEADS // DEC_HEAD_GROUP)) * DEC_HEAD_GROUP

    def quarter_body(groups):
        dec = _DecodeGroup(head_base, dq_ref, dkn_ref, dvn_ref, dkt_ref, dvt_ref, dbias_ref, dbiasn_ref,
                           do_ref)
        dec.scores()
        attend(groups[0])
        dec.outputs()
        for g in groups[1:]:
            attend(g)

    @pl.when(quarter == 0)
    def _():
        quarter_body([pat16(0), pat16(1), pat1(0)])

    @pl.when(quarter == 1)
    def _():
        quarter_body([pat16(2), pat16(3), pat1(1)])

    @pl.when(quarter == 2)
    def _():
        quarter_body([first4 + [unit1(0, True)], rest4[:6], rest4[6:]])

    @pl.when(quarter == 3)
    def _():
        quarter_body([pat1(2)])
        lax.fori_loop(0, SUPER // 256, merge, 0)
        kp_ref[...] = kc_ref[...]
        vp_ref[...] = vc_ref[...]


PROMPT_QUARTERS = 4


def _prompt_attention(q, k, v, decode, batch0, nbatch):
    s = q.shape[0]
    assert s % SUPER == 0
    bias = jnp.asarray(_prompt_bias_table())
    cur = lambda p, b, c: (b, p)
    blk = (SUPER, LANES)
    nblk = s // SUPER
    grid = (ATT_WIDTH // LANES, nblk, PROMPT_QUARTERS)
    step_of = lambda p, b, c: (p * nblk + b) * PROMPT_QUARTERS + c
    d_specs, d_operands, d_out_spec, d_out_shape = _decode_host_specs(
        decode, batch0, nbatch, grid[0] * grid[1] * grid[2], step_of)
    return pl.pallas_call(
        _prompt_attn_kernel,
        grid=grid,
        in_specs=[
            pl.BlockSpec(blk, cur),
            pl.BlockSpec(blk, cur),
            pl.BlockSpec(blk, cur),
            pl.BlockSpec((2, 3, 2, Q_TILE, 2 * Q_TILE), lambda p, b, c: (0, 0, p, 0, 0)),
        ] + d_specs,
        out_specs=[pl.BlockSpec(blk, cur), d_out_spec],
        out_shape=[jax.ShapeDtypeStruct((s, ATT_WIDTH), BF16), d_out_shape],
        scratch_shapes=[pltpu.VMEM((3, SUPER, LANES), F32)] * 3 + [pltpu.VMEM(blk, F32)] * 2,
        compiler_params=pltpu.CompilerParams(
            dimension_semantics=("arbitrary", "arbitrary", "arbitrary"), vmem_limit_bytes=VMEM_LIMIT),
        name="prompt_attention_with_decode_attention",
    )(q, k, v, bias, *d_operands)


DEC_SEQ = 8
DEC_HEAD_GROUP = 8


def _decode_bias_tables(past_len):
    dist_max = MAX_WINDOW
    mult = np.zeros(dist_max + 1, np.float64)
    for w, d in zip(WINDOWS, DILATIONS):
        mult[np.arange(w // d + 1) * d] += 1.0
    slopes = _alibi_slopes()

    def table(dist):
        ok = (dist >= 0) & (dist <= dist_max)
        dc = np.clip(dist, 0, dist_max)
        mu = np.where(ok, mult[dc], 0.0)
        with np.errstate(divide="ignore"):
            logm = np.where(mu > 0, np.log(np.maximum(mu, 1e-30)), -np.inf)
        return ((-slopes[:, None, None] * dc[None].astype(np.float64) + logm[None]) * LOG2E).astype(np.float32)

    qpos = past_len + np.arange(DEC_SEQ)
    old = table(qpos[:, None] - np.arange(past_len)[None, :])
    new = np.full((N_HEADS, DEC_SEQ, 2 * DEC_SEQ), -np.inf, np.float32)
    new[:, :, :DEC_SEQ] = table(qpos[:, None] - qpos[None, :])
    return old, new


def _pad16(a):
    return jnp.concatenate([a, jnp.zeros_like(a)], axis=0).astype(BF16)


class _DecodeGroup:
    def __init__(self, head_base, q_ref, kn_ref, vn_ref, kt_ref, vt_ref, bias_ref, biasn_ref, o_ref):
        self.head_base = head_base
        self.refs = (q_ref, kn_ref, vn_ref, kt_ref, vt_ref, bias_ref, biasn_ref, o_ref)

    def scores(self):
        q_ref, kn_ref, _, kt_ref, _, bias_ref, biasn_ref, _ = self.refs
        self.s = []
        for u in range(DEC_HEAD_GROUP):
            c = slice(u * HEAD_DIM, (u + 1) * HEAD_DIM)
            q = _pad16(q_ref[:, c])
            s = _dot(q, kt_ref[u].astype(BF16))[:DEC_SEQ] + bias_ref[self.head_base + u]
            sn = _dot_nt(q, _pad16(kn_ref[:, c]))[:DEC_SEQ] + biasn_ref[self.head_base + u]
            self.s.append((s, sn))

    def outputs(self):
        _, _, vn_ref, _, vt_ref, _, _, o_ref = self.refs
        probs = []
        for s, sn in self.s:
            m = jnp.maximum(jnp.max(s, axis=1, keepdims=True), jnp.max(sn, axis=1, keepdims=True))
            probs.append((jnp.exp2(s - m), jnp.exp2(sn - m)))
        for u, (p, pn) in enumerate(probs):
            c = slice(u * HEAD_DIM, (u + 1) * HEAD_DIM)
            l = jnp.sum(p, axis=1, keepdims=True) + jnp.sum(pn, axis=1, keepdims=True)
            o = (_dot_nt(_pad16(p), vt_ref[u].astype(BF16))[:DEC_SEQ]
                 + _dot(_pad16(pn), _pad16(vn_ref[:, c]))[:DEC_SEQ])
            o_ref[:, c] = o / l


def _decode_host_specs(decode, batch0, nbatch, nsteps, step_of):
    q, kn, vn, kt, vt = decode
    nb, _, _, past = kt.shape
    gpr = N_HEADS // DEC_HEAD_GROUP
    gw = DEC_HEAD_GROUP * HEAD_DIM
    assert q.shape == (nb, DEC_SEQ, ATT_WIDTH) and nsteps == nbatch * gpr and batch0 + nbatch <= nb
    old, new = _decode_bias_tables(past)
    small = pl.BlockSpec((None, DEC_SEQ, gw),
                         lambda *g: (batch0 + step_of(*g) // gpr, 0, step_of(*g) % gpr))
    big = pl.BlockSpec((None, DEC_HEAD_GROUP, HEAD_DIM, past),
                       lambda *g: (batch0 + step_of(*g) // gpr, step_of(*g) % gpr, 0, 0))
    const3 = lambda *g: (0, 0, 0)
    in_specs = [small, small, small, big, big,
                pl.BlockSpec((N_HEADS, DEC_SEQ, past), const3),
                pl.BlockSpec((N_HEADS, DEC_SEQ, 2 * DEC_SEQ), const3)]
    out_spec = pl.BlockSpec((None, DEC_SEQ, gw), lambda *g: (step_of(*g) // gpr, 0, step_of(*g) % gpr))
    out_shape = jax.ShapeDtypeStruct((nbatch, DEC_SEQ, ATT_WIDTH), F32)
    return in_specs, (q, kn, vn, kt, vt, jnp.asarray(old), jnp.asarray(new)), out_spec, out_shape


def _outproj_kernel(x_ref, att_ref, cz_ref, wo_ref, g2_ref, x1_ref, h2_ref):
    mixed = jnp.concatenate([att_ref[...].astype(BF16), cz_ref[...]], axis=1)
    y = x_ref[...] + _dot(mixed, wo_ref[...])
    x1_ref[...] = y
    ms = jnp.mean(y * y, axis=-1, keepdims=True)
    h2_ref[...] = (y * lax.rsqrt(ms + EPS) * g2_ref[...]).astype(BF16)


def _outproj(x, att, cz, w_out, g2, *, tm=OUTPROJ_ROWS):
    m = x.shape[0]
    assert m % tm == 0
    row = lambda i: (i, 0)
    const = lambda i: (0, 0)
    return pl.pallas_call(
        _outproj_kernel,
        grid=(m // tm,),
        in_specs=[
            pl.BlockSpec((tm, D_MODEL), row),
            pl.BlockSpec((tm, ATT_WIDTH), row),
            pl.BlockSpec((tm, CONV_WIDTH), row),
            pl.BlockSpec((D_MODEL, D_MODEL), const),
            pl.BlockSpec((1, D_MODEL), const),
        ],
        out_specs=[pl.BlockSpec((tm, D_MODEL), row), pl.BlockSpec((tm, D_MODEL), row)],
        out_shape=[jax.ShapeDtypeStruct((m, D_MODEL), F32), jax.ShapeDtypeStruct((m, D_MODEL), BF16)],
        compiler_params=pltpu.CompilerParams(
            dimension_semantics=("arbitrary",), vmem_limit_bytes=VMEM_LIMIT),
        name="outproj",
    )(x, att, cz, w_out, g2)


def _ffn_kernel(x1_ref, h2_ref, wu_ref, wd_ref, *rest, groups_per_row):
    y_ref = rest[-2] if groups_per_row else rest[-1]

    @pl.when(pl.program_id(1) == 0)
    def _():
        y_ref[...] = x1_ref[...]

    def ffn_rows(rows):
        a = _dot(h2_ref[rows, :], wu_ref[...])
        g = jnp.square(jnp.maximum(a, 0.0)).astype(BF16)
        y_ref[rows, :] += _dot(g, wd_ref[...])

    if not groups_per_row:
        ffn_rows(slice(None))
        return

    q_ref, kn_ref, vn_ref, kt_ref, vt_ref, bias_ref, biasn_ref, _, o_ref = rest
    step = pl.program_id(0) * pl.num_programs(1) + pl.program_id(1)
    head_base = (step % groups_per_row) * DEC_HEAD_GROUP
    dec = _DecodeGroup(head_base, q_ref, kn_ref, vn_ref, kt_ref, vt_ref, bias_ref, biasn_ref, o_ref)
    half = h2_ref.shape[0] // 2
    dec.scores()
    ffn_rows(slice(0, half))
    dec.outputs()
    ffn_rows(slice(half, 2 * half))


def _ffn(x1, h2, w_up, w_down, decode=None, batch0=0, nbatch=0, *, tm=FFN_ROWS, tf=FFN_COLS):
    m = x1.shape[0]
    assert m % tm == 0 and D_FF % tf == 0
    nf = D_FF // tf
    row = lambda i, f: (i, 0)
    once = dict(pipeline_mode=pl.Buffered(1)) if decode is not None else {}
    in_specs = [
        pl.BlockSpec((tm, D_MODEL), row, **once),
        pl.BlockSpec((tm, D_MODEL), row, **once),
        pl.BlockSpec((D_MODEL, tf), lambda i, f: (0, f)),
        pl.BlockSpec((tf, D_MODEL), lambda i, f: (f, 0)),
    ]
    y_spec = pl.BlockSpec((tm, D_MODEL), row)
    y_shape = jax.ShapeDtypeStruct((m, D_MODEL), F32)
    if decode is None:
        return pl.pallas_call(
            functools.partial(_ffn_kernel, groups_per_row=0),
            grid=(m // tm, nf), in_specs=in_specs, out_specs=y_spec, out_shape=y_shape,
            compiler_params=pltpu.CompilerParams(
                dimension_semantics=("arbitrary", "arbitrary"), vmem_limit_bytes=VMEM_LIMIT),
            name="ffn",
        )(x1, h2, w_up, w_down)

    d_specs, d_operands, d_out_spec, d_out_shape = _decode_host_specs(
        decode, batch0, nbatch, (m // tm) * nf, lambda i, f: i * nf + f)
    return pl.pallas_call(
        functools.partial(_ffn_kernel, groups_per_row=N_HEADS // DEC_HEAD_GROUP),
        grid=(m // tm, nf),
        in_specs=in_specs + d_specs,
        out_specs=[y_spec, d_out_spec],
        out_shape=[y_shape, d_out_shape],
        compiler_params=pltpu.CompilerParams(
            dimension_semantics=("arbitrary", "arbitrary"), vmem_limit_bytes=VMEM_LIMIT_HOST),
        name="ffn_with_decode_attention",
    )(x1, h2, w_up, w_down, *d_operands)


def _batch_minor_kernel(k_ref, v_ref, kt_ref, vt_ref, *, t):
    for src, dst in ((k_ref, kt_ref), (v_ref, vt_ref)):
        x3 = src[...].reshape(src.shape[0] // t, t, ATT_WIDTH)
        for tok in range(t):
            dst[tok] = x3[:, tok, :].T


def _batch_minor(k, v, nb, t):
    full = pl.BlockSpec((nb * t, ATT_WIDTH), lambda i: (0, 0))
    out = pl.BlockSpec((t, ATT_WIDTH, nb), lambda i: (0, 0, 0))
    shape = jax.ShapeDtypeStruct((t, ATT_WIDTH, nb), F32)
    return pl.pallas_call(
        functools.partial(_batch_minor_kernel, t=t),
        grid=(1,), in_specs=[full, full], out_specs=[out, out], out_shape=[shape, shape],
        compiler_params=pltpu.CompilerParams(
            dimension_semantics=("arbitrary",), vmem_limit_bytes=VMEM_LIMIT),
        name="batch_minor_rows",
    )(k, v)


def _head_sum_matrices():
    head_of_col = np.arange(ATT_WIDTH) // HEAD_DIM
    red = (head_of_col[:, None] == np.arange(LANES)[None, :]).astype(np.float32)
    return jnp.asarray(red, BF16), jnp.asarray(np.concatenate([red.T, red.T], axis=0), BF16)


def kernel(x_prompt, x_sample, state_k, state_v, state_conv, norm1_g, w_in, q_norm_g, k_norm_g,
           conv_w, w_out, norm2_g, w_up, w_down):
    depth = w_in.shape[0]
    assert depth == 1
    bp, sp, _ = x_prompt.shape
    nb, t, _ = x_sample.shape
    past = state_k.shape[2]
    assert bp == 1 and t == 8 and past == MAX_WINDOW and sp >= MAX_WINDOW

    red, expand = _head_sum_matrices()
    g1 = norm1_g[0][None]
    g2 = norm2_g[0][None]
    qg = jnp.tile(q_norm_g[0], N_HEADS)[None]
    kg = jnp.tile(k_norm_g[0], N_HEADS)[None]
    cw = conv_w[0]
    w_in_b = w_in[0].astype(BF16)

    xp = x_prompt[0]
    zero_conv = jnp.zeros((CONV_K - 1, CONV_WIDTH), F32)
    keep = min(MAX_WINDOW, sp)
    qp, kp, vp, czp, up, w_out_b, w_up_b, w_down_b, k_tail, v_tail = _inproj(
        xp, zero_conv, g1, w_in_b, qg, kg, red, expand, cw, grouped=False,
        casts=(w_out[0], w_up[0], w_down[0]), tail_rows=keep)

    xs = x_sample.reshape(nb * t, D_MODEL)
    qs, ks, vs, czs, us = _inproj(xs, state_conv[0], g1, w_in_b, qg, kg, red, expand, cw, grouped=True)
    kt = jnp.transpose(state_k[0], (0, 2, 3, 1))
    vt = jnp.transpose(state_v[0], (0, 2, 3, 1))
    per_batch = lambda a: a.reshape(nb, t, ATT_WIDTH)
    decode = (per_batch(qs), per_batch(ks), per_batch(vs), kt, vt)
    gpr = N_HEADS // DEC_HEAD_GROUP
    nb_attn = (sp // SUPER) * (ATT_WIDTH // LANES) * PROMPT_QUARTERS // gpr
    assert 0 < nb_attn < nb

    attp, atts_a = _prompt_attention(qp, kp, vp, decode, 0, nb_attn)
    x1p, h2p = _outproj(xp, attp, czp, w_out_b, g2)
    yp, atts_f = _ffn(x1p, h2p, w_up_b, w_down_b, decode, nb_attn, nb - nb_attn)
    atts = jnp.concatenate([atts_a, atts_f], axis=0).reshape(nb * t, ATT_WIDTH)
    x1s, h2s = _outproj(xs, atts, czs, w_out_b, g2)
    ys = _ffn(x1s, h2s, w_up_b, w_down_b)

    new_k_prompt = k_tail.reshape(1, 1, keep, N_HEADS, HEAD_DIM)
    new_v_prompt = v_tail.reshape(1, 1, keep, N_HEADS, HEAD_DIM)
    new_conv_prompt = up[8 - (CONV_K - 1):].reshape(1, 1, CONV_K - 1, CONV_WIDTH)
    ks_bm, vs_bm = _batch_minor(ks, vs, nb, t)
    to_rows = lambda a: jnp.transpose(a.reshape(1, t, N_HEADS, HEAD_DIM, nb), (0, 4, 1, 2, 3))
    new_k_sample = to_rows(ks_bm)
    new_v_sample = to_rows(vs_bm)
    new_conv_sample = us.reshape(nb, t, CONV_WIDTH)[:, t - (CONV_K - 1):][None]
    return (yp[None], ys.reshape(nb, t, D_MODEL), new_k_prompt, new_v_prompt, new_conv_prompt,
            new_k_sample, new_v_sample, new_conv_sample)
```

```python
import functools

import numpy as np
import jax
import jax.numpy as jnp
from jax import lax
from jax.experimental import pallas as pl
from jax.experimental.pallas import tpu as pltpu

F32 = jnp.float32
BF16 = jnp.bfloat16

D_MODEL = 2048
ATT_WIDTH = 1024
CONV_WIDTH = 1024
HEAD_DIM = 64
N_HEADS = 16
WINDOWS = (128, 512, 2048)
DILATIONS = (1, 4, 16)
MAX_WINDOW = 2048
CONV_K = 3
D_FF = 4 * D_MODEL
EPS = 1e-6
LOG2E = float(np.log2(np.e))
Q_SCALE = HEAD_DIM ** -0.5 * LOG2E

LANES = 128
Q_TILE = 128
SUPER = MAX_WINDOW
VMEM_LIMIT = 56 * 1024 * 1024
VMEM_LIMIT_HOST = 60 * 1024 * 1024

INPROJ_ROWS = 512
OUTPROJ_ROWS = 512
FFN_ROWS, FFN_COLS = 1024, 512


def _dot(a, b):
    return jnp.dot(a, b, preferred_element_type=F32)


def _dot_nt(a, b):
    return lax.dot_general(a, b, (((1,), (1,)), ((), ())), preferred_element_type=F32)


def _alibi_slopes():
    return 2.0 ** (-8.0 * np.arange(1, N_HEADS + 1, dtype=np.float64) / N_HEADS)


def _head_rmsnorm(p, gain_row, red, expand):
    ss = _dot((p * p).astype(BF16), red)
    inv = lax.rsqrt(ss * (1.0 / HEAD_DIM) + EPS)
    hi = inv.astype(BF16)
    lo = (inv - hi.astype(F32)).astype(BF16)
    inv_e = _dot(jnp.concatenate([hi, lo], axis=1), expand)
    return p * inv_e * gain_row


def _inproj_kernel(*refs, tm, grouped, n_cast, n_tail):
    x_ref, g1_ref, w_ref, qg_ref, kg_ref, red_ref, exp_ref, cw_ref, past_ref = refs[:9]
    cast_in = refs[9:9 + n_cast]
    q_ref, k_ref, v_ref, cz_ref, u_ref = refs[9 + n_cast:14 + n_cast]
    cast_out = refs[14 + n_cast:14 + 2 * n_cast]
    tails = refs[14 + 2 * n_cast:14 + 2 * n_cast + n_tail]
    h_scr, gb_scr, c_scr, ubuf = refs[14 + 2 * n_cast + n_tail:]
    i = pl.program_id(0)
    j = pl.program_id(1)

    @pl.when(j == 0)
    def _():
        x = x_ref[...]
        ms = jnp.mean(x * x, axis=-1, keepdims=True)
        h_scr[...] = (x * lax.rsqrt(ms + EPS) * g1_ref[...]).astype(BF16)

    proj = _dot(h_scr[...], w_ref[...])

    for src, dst in zip(cast_in, cast_out):
        dst[...] = src[...].astype(BF16)

    @pl.when(j == 0)
    def _():
        q_ref[...] = _head_rmsnorm(proj, qg_ref[...], red_ref[...], exp_ref[...]) * Q_SCALE

    @pl.when(j == 1)
    def _():
        k = _head_rmsnorm(proj, kg_ref[...], red_ref[...], exp_ref[...])
        k_ref[...] = k
        if tails:
            tails[0][...] = k

    @pl.when(j == 2)
    def _():
        v_ref[...] = proj
        if tails:
            tails[1][...] = proj

    @pl.when(j == 3)
    def _():
        gb_scr[...] = proj

    @pl.when(j == 4)
    def _():
        c_scr[...] = proj

    @pl.when(j == 5)
    def _():
        u = c_scr[...] * proj
        cw = cw_ref[...]
        if grouped:
            g = tm // 8
            u3 = u.reshape(g, 8, CONV_WIDTH)
            past = past_ref[...]
            p0 = past[:, 0:1, :]
            p1 = past[:, 1:2, :]
            tok = lax.broadcasted_iota(jnp.int32, u3.shape, 1)
            prev1 = jnp.where(tok == 0, p1, pltpu.roll(u3, 1, axis=1))
            prev2 = jnp.where(tok == 0, p0, jnp.where(tok == 1, p1, pltpu.roll(u3, 2, axis=1)))
            conv = cw[0:1][None] * prev2 + cw[1:2][None] * prev1 + cw[2:3][None] * u3
            cz_ref[...] = (gb_scr[...] * conv.reshape(tm, CONV_WIDTH)).astype(BF16)
            u_ref[...] = u
        else:
            @pl.when(i == 0)
            def _():
                ubuf[0:8, :] = jnp.zeros((8, CONV_WIDTH), F32)
                ubuf[6:8, :] = past_ref[...]

            ubuf[8:tm + 8, :] = u
            conv = cw[0:1] * ubuf[6:tm + 6, :] + cw[1:2] * ubuf[7:tm + 7, :] + cw[2:3] * u
            cz_ref[...] = (gb_scr[...] * conv).astype(BF16)
            tail = ubuf[tm:tm + 8, :]
            ubuf[0:8, :] = tail
            u_ref[...] = tail


CAST_SLABS = 64


def _inproj(x, past, g1, w_in, qg, kg, red, expand, cw, *, grouped, casts=(), tail_rows=0,
            tm=INPROJ_ROWS):
    m = x.shape[0]
    assert m % tm == 0 and tail_rows % tm == 0
    nt = m // tm
    row = lambda i, j: (i, 0)
    const2 = lambda i, j: (0, 0)
    tail_map = lambda i, j: (jnp.maximum(i - (nt - tail_rows // tm), 0), 0)
    tail_specs = [pl.BlockSpec((tm, ATT_WIDTH), tail_map)] * 2 if tail_rows else []
    tail_shapes = [jax.ShapeDtypeStruct((tail_rows, ATT_WIDTH), F32)] * 2 if tail_rows else []
    assert not casts or nt * 6 >= CAST_SLABS
    slab = lambda i, j: (jnp.minimum(i * 6 + j, CAST_SLABS - 1), 0)
    cast_specs = [pl.BlockSpec((a.shape[0] // CAST_SLABS, a.shape[1]), slab) for a in casts]
    cast_shapes = [jax.ShapeDtypeStruct(a.shape, BF16) for a in casts]
    assert all(a.shape[0] % (16 * CAST_SLABS) == 0 for a in casts)
    if grouped:
        past_spec = pl.BlockSpec((tm // 8, CONV_K - 1, CONV_WIDTH), lambda i, j: (i, 0, 0))
        u_shape = jax.ShapeDtypeStruct((m, CONV_WIDTH), F32)
        u_spec = pl.BlockSpec((tm, CONV_WIDTH), row)
    else:
        past_spec = pl.BlockSpec((CONV_K - 1, CONV_WIDTH), const2)
        u_shape = jax.ShapeDtypeStruct((8, CONV_WIDTH), F32)
        u_spec = pl.BlockSpec((8, CONV_WIDTH), const2)
    out_f32 = jax.ShapeDtypeStruct((m, ATT_WIDTH), F32)
    return pl.pallas_call(
        functools.partial(_inproj_kernel, tm=tm, grouped=grouped, n_cast=len(casts),
                          n_tail=len(tail_specs)),
        grid=(nt, 6),
        in_specs=[
            pl.BlockSpec((tm, D_MODEL), row),
            pl.BlockSpec((1, D_MODEL), const2),
            pl.BlockSpec((D_MODEL, 1024), lambda i, j: (0, j)),
            pl.BlockSpec((1, ATT_WIDTH), const2),
            pl.BlockSpec((1, ATT_WIDTH), const2),
            pl.BlockSpec((ATT_WIDTH, LANES), const2),
            pl.BlockSpec((2 * LANES, ATT_WIDTH), const2),
            pl.BlockSpec((CONV_K, CONV_WIDTH), const2),
            past_spec,
        ] + cast_specs,
        out_specs=[
            pl.BlockSpec((tm, ATT_WIDTH), row),
            pl.BlockSpec((tm, ATT_WIDTH), row),
            pl.BlockSpec((tm, ATT_WIDTH), row),
            pl.BlockSpec((tm, CONV_WIDTH), row),
            u_spec,
        ] + cast_specs + tail_specs,
        out_shape=[out_f32, out_f32, out_f32,
                   jax.ShapeDtypeStruct((m, CONV_WIDTH), BF16), u_shape] + cast_shapes + tail_shapes,
        scratch_shapes=[
            pltpu.VMEM((tm, D_MODEL), BF16),
            pltpu.VMEM((tm, CONV_WIDTH), F32),
            pltpu.VMEM((tm, CONV_WIDTH), F32),
            pltpu.VMEM((tm + 8, CONV_WIDTH), F32),
        ],
        compiler_params=pltpu.CompilerParams(
            dimension_semantics=("arbitrary", "arbitrary"),
            vmem_limit_bytes=VMEM_LIMIT_HOST if tail_rows else VMEM_LIMIT),
        name="inproj_grouped" if grouped else "inproj_seq",
    )(x, g1, w_in, qg, kg, red, expand, cw, past, *casts)


def _prompt_bias_table():
    qi = np.arange(Q_TILE)[:, None]
    kc = np.arange(2 * Q_TILE)[None, :]
    jdist = (Q_TILE + qi - kc).astype(np.float64)
    valid = (jdist >= 0) & (jdist <= Q_TILE)
    slopes = _alibi_slopes()
    tab = np.empty((2, len(DILATIONS), N_HEADS, Q_TILE, 2 * Q_TILE), np.float32)
    for f in range(2):
        ok = valid & (kc >= Q_TILE) if f else valid
        for p, d in enumerate(DILATIONS):
            for h in range(N_HEADS):
                tab[f, p, h] = np.where(ok, -slopes[h] * d * jdist * LOG2E, -np.inf)
    return tab


def _prompt_attn_kernel(q_ref, kc_ref, vc_ref, bias_ref,
                        dq_ref, dkn_ref, dvn_ref, dkt_hbm, dvt_hbm, dbias_ref, dbiasn_ref,
                        o_ref, do_ref, o_scr, m_scr, l_scr, kp_ref, vp_ref,
                        dkbuf, dvbuf, dksem, dvsem, *, batch0):
    first = jnp.where(pl.program_id(1) == 0, 1, 0)
    quarter = pl.program_id(2)
    step = (pl.program_id(0) * pl.num_programs(1) + pl.program_id(1)) * pl.num_programs(2) + quarter
    nsteps = pl.num_programs(0) * pl.num_programs(1) * pl.num_programs(2)
    dkt_ref, dvt_ref = _WindowRing(dkt_hbm, dvt_hbm, dkbuf, dvbuf, dksem, dvsem, batch0).advance(
        step, nsteps)

    @pl.when((pl.program_id(1) == 0) & (quarter == 0))
    def _():
        kp_ref[...] = jnp.zeros(kp_ref.shape, F32)
        vp_ref[...] = jnp.zeros(vp_ref.shape, F32)
    lane = lax.broadcasted_iota(jnp.int32, (Q_TILE, LANES), 1)
    lo = lane < HEAD_DIM
    hi = jnp.logical_not(lo)

    def attend(units):
        def scores(unit):
            pat, variant, rows, kprev_ref, vprev_ref, prev = unit
            q = q_ref[rows, :]
            k2 = jnp.concatenate([kprev_ref[prev, :], kc_ref[rows, :]], axis=0).astype(BF16)
            q2 = jnp.concatenate([jnp.where(lo, q, 0.0), jnp.where(hi, q, 0.0)], axis=0).astype(BF16)
            return _dot_nt(q2, k2) + bias_ref[variant, pat].reshape(2 * Q_TILE, 2 * Q_TILE)

        def softmax(s):
            m = jnp.max(s, axis=1, keepdims=True)
            return m, jnp.exp2(s - m).astype(BF16)

        def output(unit, stats):
            pat, _, rows, _, vprev_ref, prev = unit
            v2 = jnp.concatenate([vprev_ref[prev, :], vc_ref[rows, :]], axis=0).astype(BF16)
            v_ext = jnp.concatenate([v2, jnp.ones_like(v2)], axis=1)
            m, p = stats
            f = _dot(p, v_ext)
            o_scr[pat, rows, :] = jnp.where(lo, f[:Q_TILE, :LANES], f[Q_TILE:, :LANES])
            m_scr[pat, rows, :] = jnp.where(lo, m[:Q_TILE], m[Q_TILE:])
            l_scr[pat, rows, :] = jnp.where(lo, f[:Q_TILE, LANES:], f[Q_TILE:, LANES:])

        all_scores = [scores(unit) for unit in units]
        all_stats = [softmax(ss) for ss in all_scores]
        for unit, stats in zip(units, all_stats):
            output(unit, stats)

    def unit16(r):
        rows = pl.ds(r, Q_TILE, stride=16)
        return (2, first, rows, kp_ref, vp_ref, rows)

    def unit4(start, from_prev_block):
        rows = pl.ds(start, Q_TILE, stride=4)
        if from_prev_block:
            return (1, first, rows, kp_ref, vp_ref, pl.ds(SUPER - 512 + start, Q_TILE, stride=4))
        return (1, 0, rows, kc_ref, vc_ref, pl.ds(start - 512, Q_TILE, stride=4))

    def unit1(start, from_prev_block):
        rows = pl.ds(start, Q_TILE)
        if from_prev_block:
            return (0, first, rows, kp_ref, vp_ref, pl.ds(SUPER - Q_TILE, Q_TILE))
        return (0, 0, rows, kc_ref, vc_ref, pl.ds(start - Q_TILE, Q_TILE))

    def pat16(g):
        return [unit16(g * 4 + u) for u in range(4)]

    first4 = [unit4(r4, True) for r4 in range(4)]
    rest4 = [unit4(512 * c + r4, False) for c in range(1, SUPER // 512) for r4 in range(4)]

    def pat1(g):
        return [unit1((1 + 5 * g + u) * Q_TILE, False) for u in range(5)]

    def merge(c, carry):
        rows = pl.ds(pl.multiple_of(c * 256, 256), 256)
        ms = [m_scr[p, rows, :] for p in range(3)]
        mx = jnp.maximum(jnp.maximum(ms[0], ms[1]), ms[2])
        num = jnp.zeros((256, LANES), F32)
        den = jnp.zeros((256, LANES), F32)
        for p in range(3):
            w = jnp.exp2(ms[p] - mx)
            num = num + w * o_scr[p, rows, :]
            den = den + w * l_scr[p, rows, :]
        o_ref[rows, :] = (num / den).astype(BF16)
        return carry

    head_base = (step % (N_HEADS // DEC_HEAD_GROUP)) * DEC_HEAD_GROUP

    def quarter_body(groups):
        dec = _DecodeGroup(head_base, dq_ref, dkn_ref, dvn_ref, dkt_ref, dvt_ref, dbias_ref, dbiasn_ref,
                           do_ref)
        dec.scores()
        attend(groups[0])
        dec.outputs()
        for g in groups[1:]:
            attend(g)

    @pl.when(quarter == 0)
    def _():
        quarter_body([pat16(0), pat16(1), pat1(0)])

    @pl.when(quarter == 1)
    def _():
        quarter_body([pat16(2), pat16(3), pat1(1)])

    @pl.when(quarter == 2)
    def _():
        quarter_body([first4 + [unit1(0, True)], rest4[:6], rest4[6:]])

    @pl.when(quarter == 3)
    def _():
        quarter_body([pat1(2)])
        lax.fori_loop(0, SUPER // 256, merge, 0)
        kp_ref[...] = kc_ref[...]
        vp_ref[...] = vc_ref[...]


PROMPT_QUARTERS = 4


def _prompt_attention(q, k, v, decode, batch0, nbatch):
    s = q.shape[0]
    assert s % SUPER == 0
    bias = jnp.asarray(_prompt_bias_table())
    cur = lambda p, b, c: (b, p)
    blk = (SUPER, LANES)
    nblk = s // SUPER
    grid = (ATT_WIDTH // LANES, nblk, PROMPT_QUARTERS)
    step_of = lambda p, b, c: (p * nblk + b) * PROMPT_QUARTERS + c
    d_specs, d_operands, d_out_spec, d_out_shape = _decode_host_specs(
        decode, batch0, nbatch, grid[0] * grid[1] * grid[2], step_of, manual_windows=True)
    window = (WINDOW_SLOTS, DEC_HEAD_GROUP, HEAD_DIM, decode[3].shape[-1])
    return pl.pallas_call(
        functools.partial(_prompt_attn_kernel, batch0=batch0),
        grid=grid,
        in_specs=[
            pl.BlockSpec(blk, cur),
            pl.BlockSpec(blk, cur),
            pl.BlockSpec(blk, cur),
            pl.BlockSpec((2, 3, 2, Q_TILE, 2 * Q_TILE), lambda p, b, c: (0, 0, p, 0, 0)),
        ] + d_specs,
        out_specs=[pl.BlockSpec(blk, cur), d_out_spec],
        out_shape=[jax.ShapeDtypeStruct((s, ATT_WIDTH), BF16), d_out_shape],
        scratch_shapes=([pltpu.VMEM((3, SUPER, LANES), F32)] * 3 + [pltpu.VMEM(blk, F32)] * 2
                        + [pltpu.VMEM(window, F32)] * 2
                        + [pltpu.SemaphoreType.DMA((WINDOW_SLOTS,))] * 2),
        compiler_params=pltpu.CompilerParams(
            dimension_semantics=("arbitrary", "arbitrary", "arbitrary"), vmem_limit_bytes=VMEM_LIMIT),
        name="prompt_attention_with_decode_attention",
    )(q, k, v, bias, *d_operands)


DEC_SEQ = 8
DEC_HEAD_GROUP = 8


def _decode_bias_tables(past_len):
    dist_max = MAX_WINDOW
    mult = np.zeros(dist_max + 1, np.float64)
    for w, d in zip(WINDOWS, DILATIONS):
        mult[np.arange(w // d + 1) * d] += 1.0
    slopes = _alibi_slopes()

    def table(dist):
        ok = (dist >= 0) & (dist <= dist_max)
        dc = np.clip(dist, 0, dist_max)
        mu = np.where(ok, mult[dc], 0.0)
        with np.errstate(divide="ignore"):
            logm = np.where(mu > 0, np.log(np.maximum(mu, 1e-30)), -np.inf)
        return ((-slopes[:, None, None] * dc[None].astype(np.float64) + logm[None]) * LOG2E).astype(np.float32)

    qpos = past_len + np.arange(DEC_SEQ)
    old = table(qpos[:, None] - np.arange(past_len)[None, :])
    new = np.full((N_HEADS, DEC_SEQ, 2 * DEC_SEQ), -np.inf, np.float32)
    new[:, :, :DEC_SEQ] = table(qpos[:, None] - qpos[None, :])
    return old, new


def _pad16(a):
    return jnp.concatenate([a, jnp.zeros_like(a)], axis=0).astype(BF16)


class _DecodeGroup:
    def __init__(self, head_base, q_ref, kn_ref, vn_ref, kt_ref, vt_ref, bias_ref, biasn_ref, o_ref):
        self.head_base = head_base
        self.refs = (q_ref, kn_ref, vn_ref, kt_ref, vt_ref, bias_ref, biasn_ref, o_ref)

    def scores(self):
        q_ref, kn_ref, _, kt_ref, _, bias_ref, biasn_ref, _ = self.refs
        self.s = []
        for u in range(DEC_HEAD_GROUP):
            c = slice(u * HEAD_DIM, (u + 1) * HEAD_DIM)
            q = _pad16(q_ref[:, c])
            s = _dot(q, kt_ref[u].astype(BF16))[:DEC_SEQ] + bias_ref[self.head_base + u]
            sn = _dot_nt(q, _pad16(kn_ref[:, c]))[:DEC_SEQ] + biasn_ref[self.head_base + u]
            self.s.append((s, sn))

    def outputs(self):
        _, _, vn_ref, _, vt_ref, _, _, o_ref = self.refs
        probs = []
        for s, sn in self.s:
            m = jnp.maximum(jnp.max(s, axis=1, keepdims=True), jnp.max(sn, axis=1, keepdims=True))
            probs.append((jnp.exp2(s - m), jnp.exp2(sn - m)))
        for u, (p, pn) in enumerate(probs):
            c = slice(u * HEAD_DIM, (u + 1) * HEAD_DIM)
            l = jnp.sum(p, axis=1, keepdims=True) + jnp.sum(pn, axis=1, keepdims=True)
            o = (_dot_nt(_pad16(p), vt_ref[u].astype(BF16))[:DEC_SEQ]
                 + _dot(_pad16(pn), _pad16(vn_ref[:, c]))[:DEC_SEQ])
            o_ref[:, c] = o / l


WINDOW_SLOTS = 3


class _WindowRing:
    def __init__(self, kt_hbm, vt_hbm, kbuf, vbuf, ksem, vsem, batch0):
        self.arrays = ((kt_hbm, kbuf, ksem), (vt_hbm, vbuf, vsem))
        self.batch0 = batch0

    def _copies(self, unit):
        gpr = N_HEADS // DEC_HEAD_GROUP
        batch = self.batch0 + unit // gpr
        heads = pl.ds((unit % gpr) * DEC_HEAD_GROUP, DEC_HEAD_GROUP)
        slot = unit % WINDOW_SLOTS
        return [pltpu.make_async_copy(hbm.at[batch, heads], buf.at[slot], sem.at[slot])
                for hbm, buf, sem in self.arrays]

    def advance(self, step, nsteps):
        @pl.when(step == 0)
        def _():
            for ahead in range(WINDOW_SLOTS - 1):
                for c in self._copies(ahead):
                    c.start()

        @pl.when(step + WINDOW_SLOTS - 1 < nsteps)
        def _():
            for c in self._copies(step + WINDOW_SLOTS - 1):
                c.start()

        for c in self._copies(step):
            c.wait()
        slot = step % WINDOW_SLOTS
        return self.arrays[0][1].at[slot], self.arrays[1][1].at[slot]


def _decode_host_specs(decode, batch0, nbatch, nsteps, step_of, manual_windows=False):
    q, kn, vn, kt, vt = decode
    nb, _, _, past = kt.shape
    gpr = N_HEADS // DEC_HEAD_GROUP
    gw = DEC_HEAD_GROUP * HEAD_DIM
    assert q.shape == (nb, DEC_SEQ, ATT_WIDTH) and nsteps == nbatch * gpr and batch0 + nbatch <= nb
    old, new = _decode_bias_tables(past)
    small = pl.BlockSpec((None, DEC_SEQ, gw),
                         lambda *g: (batch0 + step_of(*g) // gpr, 0, step_of(*g) % gpr))
    if manual_windows:
        big = pl.BlockSpec(memory_space=pl.ANY)
    else:
        big = pl.BlockSpec((None, DEC_HEAD_GROUP, HEAD_DIM, past),
                           lambda *g: (batch0 + step_of(*g) // gpr, step_of(*g) % gpr, 0, 0))
    const3 = lambda *g: (0, 0, 0)
    in_specs = [small, small, small, big, big,
                pl.BlockSpec((N_HEADS, DEC_SEQ, past), const3),
                pl.BlockSpec((N_HEADS, DEC_SEQ, 2 * DEC_SEQ), const3)]
    out_spec = pl.BlockSpec((None, DEC_SEQ, gw), lambda *g: (step_of(*g) // gpr, 0, step_of(*g) % gpr))
    out_shape = jax.ShapeDtypeStruct((nbatch, DEC_SEQ, ATT_WIDTH), F32)
    return in_specs, (q, kn, vn, kt, vt, jnp.asarray(old), jnp.asarray(new)), out_spec, out_shape


def _outproj_kernel(x_ref, att_ref, cz_ref, wo_ref, g2_ref, x1_ref, h2_ref):
    mixed = jnp.concatenate([att_ref[...].astype(BF16), cz_ref[...]], axis=1)
    y = x_ref[...] + _dot(mixed, wo_ref[...])
    x1_ref[...] = y
    ms = jnp.mean(y * y, axis=-1, keepdims=True)
    h2_ref[...] = (y * lax.rsqrt(ms + EPS) * g2_ref[...]).astype(BF16)


def _outproj(x, att, cz, w_out, g2, *, tm=OUTPROJ_ROWS):
    m = x.shape[0]
    assert m % tm == 0
    row = lambda i: (i, 0)
    const = lambda i: (0, 0)
    return pl.pallas_call(
        _outproj_kernel,
        grid=(m // tm,),
        in_specs=[
            pl.BlockSpec((tm, D_MODEL), row),
            pl.BlockSpec((tm, ATT_WIDTH), row),
            pl.BlockSpec((tm, CONV_WIDTH), row),
            pl.BlockSpec((D_MODEL, D_MODEL), const),
            pl.BlockSpec((1, D_MODEL), const),
        ],
        out_specs=[pl.BlockSpec((tm, D_MODEL), row), pl.BlockSpec((tm, D_MODEL), row)],
        out_shape=[jax.ShapeDtypeStruct((m, D_MODEL), F32), jax.ShapeDtypeStruct((m, D_MODEL), BF16)],
        compiler_params=pltpu.CompilerParams(
            dimension_semantics=("arbitrary",), vmem_limit_bytes=VMEM_LIMIT),
        name="outproj",
    )(x, att, cz, w_out, g2)


def _ffn_kernel(x1_ref, h2_ref, wu_ref, wd_ref, *rest, groups_per_row):
    y_ref = rest[-2] if groups_per_row else rest[-1]

    @pl.when(pl.program_id(1) == 0)
    def _():
        y_ref[...] = x1_ref[...]

    def ffn_rows(rows):
        a = _dot(h2_ref[rows, :], wu_ref[...])
        g = jnp.square(jnp.maximum(a, 0.0)).astype(BF16)
        y_ref[rows, :] += _dot(g, wd_ref[...])

    if not groups_per_row:
        ffn_rows(slice(None))
        return

    q_ref, kn_ref, vn_ref, kt_ref, vt_ref, bias_ref, biasn_ref, _, o_ref = rest
    step = pl.program_id(0) * pl.num_programs(1) + pl.program_id(1)
    head_base = (step % groups_per_row) * DEC_HEAD_GROUP
    dec = _DecodeGroup(head_base, q_ref, kn_ref, vn_ref, kt_ref, vt_ref, bias_ref, biasn_ref, o_ref)
    half = h2_ref.shape[0] // 2
    dec.scores()
    ffn_rows(slice(0, half))
    dec.outputs()
    ffn_rows(slice(half, 2 * half))


def _ffn(x1, h2, w_up, w_down, decode=None, batch0=0, nbatch=0, *, tm=FFN_ROWS, tf=FFN_COLS):
    m = x1.shape[0]
    assert m % tm == 0 and D_FF % tf == 0
    nf = D_FF // tf
    row = lambda i, f: (i, 0)
    once = dict(pipeline_mode=pl.Buffered(1)) if decode is not None else {}
    in_specs = [
        pl.BlockSpec((tm, D_MODEL), row, **once),
        pl.BlockSpec((tm, D_MODEL), row, **once),
        pl.BlockSpec((D_MODEL, tf), lambda i, f: (0, f)),
        pl.BlockSpec((tf, D_MODEL), lambda i, f: (f, 0)),
    ]
    y_spec = pl.BlockSpec((tm, D_MODEL), row)
    y_shape = jax.ShapeDtypeStruct((m, D_MODEL), F32)
    if decode is None:
        return pl.pallas_call(
            functools.partial(_ffn_kernel, groups_per_row=0),
            grid=(m // tm, nf), in_specs=in_specs, out_specs=y_spec, out_shape=y_shape,
            compiler_params=pltpu.CompilerParams(
                dimension_semantics=("arbitrary", "arbitrary"), vmem_limit_bytes=VMEM_LIMIT),
            name="ffn",
        )(x1, h2, w_up, w_down)

    d_specs, d_operands, d_out_spec, d_out_shape = _decode_host_specs(
        decode, batch0, nbatch, (m // tm) * nf, lambda i, f: i * nf + f)
    return pl.pallas_call(
        functools.partial(_ffn_kernel, groups_per_row=N_HEADS // DEC_HEAD_GROUP),
        grid=(m // tm, nf),
        in_specs=in_specs + d_specs,
        out_specs=[y_spec, d_out_spec],
        out_shape=[y_shape, d_out_shape],
        compiler_params=pltpu.CompilerParams(
            dimension_semantics=("arbitrary", "arbitrary"), vmem_limit_bytes=VMEM_LIMIT_HOST),
        name="ffn_with_decode_attention",
    )(x1, h2, w_up, w_down, *d_operands)


def _batch_minor_kernel(k_ref, v_ref, kt_ref, vt_ref, *, t):
    for src, dst in ((k_ref, kt_ref), (v_ref, vt_ref)):
        x3 = src[...].reshape(src.shape[0] // t, t, ATT_WIDTH)
        for tok in range(t):
            dst[tok] = x3[:, tok, :].T


def _batch_minor(k, v, nb, t):
    full = pl.BlockSpec((nb * t, ATT_WIDTH), lambda i: (0, 0))
    out = pl.BlockSpec((t, ATT_WIDTH, nb), lambda i: (0, 0, 0))
    shape = jax.ShapeDtypeStruct((t, ATT_WIDTH, nb), F32)
    return pl.pallas_call(
        functools.partial(_batch_minor_kernel, t=t),
        grid=(1,), in_specs=[full, full], out_specs=[out, out], out_shape=[shape, shape],
        compiler_params=pltpu.CompilerParams(
            dimension_semantics=("arbitrary",), vmem_limit_bytes=VMEM_LIMIT),
        name="batch_minor_rows",
    )(k, v)


def _head_sum_matrices():
    head_of_col = np.arange(ATT_WIDTH) // HEAD_DIM
    red = (head_of_col[:, None] == np.arange(LANES)[None, :]).astype(np.float32)
    return jnp.asarray(red, BF16), jnp.asarray(np.concatenate([red.T, red.T], axis=0), BF16)


def kernel(x_prompt, x_sample, state_k, state_v, state_conv, norm1_g, w_in, q_norm_g, k_norm_g,
           conv_w, w_out, norm2_g, w_up, w_down):
    depth = w_in.shape[0]
    assert depth == 1
    bp, sp, _ = x_prompt.shape
    nb, t, _ = x_sample.shape
    past = state_k.shape[2]
    assert bp == 1 and t == 8 and past == MAX_WINDOW and sp >= MAX_WINDOW

    red, expand = _head_sum_matrices()
    g1 = norm1_g[0][None]
    g2 = norm2_g[0][None]
    qg = jnp.tile(q_norm_g[0], N_HEADS)[None]
    kg = jnp.tile(k_norm_g[0], N_HEADS)[None]
    cw = conv_w[0]
    w_in_b = w_in[0].astype(BF16)

    xp = x_prompt[0]
    zero_conv = jnp.zeros((CONV_K - 1, CONV_WIDTH), F32)
    keep = min(MAX_WINDOW, sp)
    qp, kp, vp, czp, up, w_out_b, w_up_b, w_down_b, k_tail, v_tail = _inproj(
        xp, zero_conv, g1, w_in_b, qg, kg, red, expand, cw, grouped=False,
        casts=(w_out[0], w_up[0], w_down[0]), tail_rows=keep)

    xs = x_sample.reshape(nb * t, D_MODEL)
    qs, ks, vs, czs, us = _inproj(xs, state_conv[0], g1, w_in_b, qg, kg, red, expand, cw, grouped=True)
    kt = jnp.transpose(state_k[0], (0, 2, 3, 1))
    vt = jnp.transpose(state_v[0], (0, 2, 3, 1))
    per_batch = lambda a: a.reshape(nb, t, ATT_WIDTH)
    decode = (per_batch(qs), per_batch(ks), per_batch(vs), kt, vt)
    gpr = N_HEADS // DEC_HEAD_GROUP
    nb_attn = (sp // SUPER) * (ATT_WIDTH // LANES) * PROMPT_QUARTERS // gpr
    assert 0 < nb_attn < nb

    attp, atts_a = _prompt_attention(qp, kp, vp, decode, 0, nb_attn)
    x1p, h2p = _outproj(xp, attp, czp, w_out_b, g2)
    yp, atts_f = _ffn(x1p, h2p, w_up_b, w_down_b, decode, nb_attn, nb - nb_attn)
    atts = jnp.concatenate([atts_a, atts_f], axis=0).reshape(nb * t, ATT_WIDTH)
    x1s, h2s = _outproj(xs, atts, czs, w_out_b, g2)
    ys = _ffn(x1s, h2s, w_up_b, w_down_b)

    new_k_prompt = k_tail.reshape(1, 1, keep, N_HEADS, HEAD_DIM)
    new_v_prompt = v_tail.reshape(1, 1, keep, N_HEADS, HEAD_DIM)
    new_conv_prompt = up[8 - (CONV_K - 1):].reshape(1, 1, CONV_K - 1, CONV_WIDTH)
    ks_bm, vs_bm = _batch_minor(ks, vs, nb, t)
    to_rows = lambda a: jnp.transpose(a.reshape(1, t, N_HEADS, HEAD_DIM, nb), (0, 4, 1, 2, 3))
    new_k_sample = to_rows(ks_bm)
    new_v_sample = to_rows(vs_bm)
    new_conv_sample = us.reshape(nb, t, CONV_WIDTH)[:, t - (CONV_K - 1):][None]
    return (yp[None], ys.reshape(nb, t, D_MODEL), new_k_prompt, new_v_prompt, new_conv_prompt,
            new_k_sample, new_v_sample, new_conv_sample)
```

```python
import functools

import numpy as np
import jax
import jax.numpy as jnp
from jax import lax
from jax.experimental import pallas as pl
from jax.experimental.pallas import tpu as pltpu

F32 = jnp.float32
BF16 = jnp.bfloat16

D_MODEL = 2048
ATT_WIDTH = 1024
CONV_WIDTH = 1024
HEAD_DIM = 64
N_HEADS = 16
WINDOWS = (128, 512, 2048)
DILATIONS = (1, 4, 16)
MAX_WINDOW = 2048
CONV_K = 3
D_FF = 4 * D_MODEL
EPS = 1e-6
LOG2E = float(np.log2(np.e))
Q_SCALE = HEAD_DIM ** -0.5 * LOG2E

LANES = 128
Q_TILE = 128
SUPER = MAX_WINDOW
VMEM_LIMIT = 56 * 1024 * 1024
VMEM_LIMIT_HOST = 60 * 1024 * 1024

INPROJ_ROWS = 512
OUTPROJ_ROWS = 512
FFN_ROWS, FFN_COLS = 1024, 512


def _dot(a, b):
    return jnp.dot(a, b, preferred_element_type=F32)


def _dot_nt(a, b):
    return lax.dot_general(a, b, (((1,), (1,)), ((), ())), preferred_element_type=F32)


def _alibi_slopes():
    return 2.0 ** (-8.0 * np.arange(1, N_HEADS + 1, dtype=np.float64) / N_HEADS)


def _head_rmsnorm(p, gain_row, red, expand):
    ss = _dot((p * p).astype(BF16), red)
    inv = lax.rsqrt(ss * (1.0 / HEAD_DIM) + EPS)
    hi = inv.astype(BF16)
    lo = (inv - hi.astype(F32)).astype(BF16)
    inv_e = _dot(jnp.concatenate([hi, lo], axis=1), expand)
    return p * inv_e * gain_row


def _inproj_kernel(*refs, tm, grouped, n_cast, n_tail):
    x_ref, g1_ref, w_ref, qg_ref, kg_ref, red_ref, exp_ref, cw_ref, past_ref = refs[:9]
    cast_in = refs[9:9 + n_cast]
    q_ref, k_ref, v_ref, cz_ref, u_ref = refs[9 + n_cast:14 + n_cast]
    cast_out = refs[14 + n_cast:14 + 2 * n_cast]
    tails = refs[14 + 2 * n_cast:14 + 2 * n_cast + n_tail]
    h_scr, gb_scr, c_scr, ubuf = refs[14 + 2 * n_cast + n_tail:]
    i = pl.program_id(0)
    j = pl.program_id(1)

    @pl.when(j == 0)
    def _():
        x = x_ref[...]
        ms = jnp.mean(x * x, axis=-1, keepdims=True)
        h_scr[...] = (x * lax.rsqrt(ms + EPS) * g1_ref[...]).astype(BF16)

    proj = _dot(h_scr[...], w_ref[...])

    for src, dst in zip(cast_in, cast_out):
        dst[...] = src[...].astype(BF16)

    @pl.when(j == 0)
    def _():
        q_ref[...] = _head_rmsnorm(proj, qg_ref[...], red_ref[...], exp_ref[...]) * Q_SCALE

    @pl.when(j == 1)
    def _():
        k = _head_rmsnorm(proj, kg_ref[...], red_ref[...], exp_ref[...])
        k_ref[...] = k
        if tails:
            tails[0][...] = k

    @pl.when(j == 2)
    def _():
        v_ref[...] = proj
        if tails:
            tails[1][...] = proj

    @pl.when(j == 3)
    def _():
        gb_scr[...] = proj

    @pl.when(j == 4)
    def _():
        c_scr[...] = proj

    @pl.when(j == 5)
    def _():
        u = c_scr[...] * proj
        cw = cw_ref[...]
        if grouped:
            g = tm // 8
            u3 = u.reshape(g, 8, CONV_WIDTH)
            past = past_ref[...]
            p0 = past[:, 0:1, :]
            p1 = past[:, 1:2, :]
            tok = lax.broadcasted_iota(jnp.int32, u3.shape, 1)
            prev1 = jnp.where(tok == 0, p1, pltpu.roll(u3, 1, axis=1))
            prev2 = jnp.where(tok == 0, p0, jnp.where(tok == 1, p1, pltpu.roll(u3, 2, axis=1)))
            conv = cw[0:1][None] * prev2 + cw[1:2][None] * prev1 + cw[2:3][None] * u3
            cz_ref[...] = (gb_scr[...] * conv.reshape(tm, CONV_WIDTH)).astype(BF16)
            u_ref[...] = u
        else:
            @pl.when(i == 0)
            def _():
                ubuf[0:8, :] = jnp.zeros((8, CONV_WIDTH), F32)
                ubuf[6:8, :] = past_ref[...]

            ubuf[8:tm + 8, :] = u
            conv = cw[0:1] * ubuf[6:tm + 6, :] + cw[1:2] * ubuf[7:tm + 7, :] + cw[2:3] * u
            cz_ref[...] = (gb_scr[...] * conv).astype(BF16)
            tail = ubuf[tm:tm + 8, :]
            ubuf[0:8, :] = tail
            u_ref[...] = tail


CAST_SLABS = 64


def _inproj(x, past, g1, w_in, qg, kg, red, expand, cw, *, grouped, casts=(), tail_rows=0,
            tm=INPROJ_ROWS):
    m = x.shape[0]
    assert m % tm == 0 and tail_rows % tm == 0
    nt = m // tm
    row = lambda i, j: (i, 0)
    const2 = lambda i, j: (0, 0)
    tail_map = lambda i, j: (jnp.maximum(i - (nt - tail_rows // tm), 0), 0)
    tail_specs = [pl.BlockSpec((tm, ATT_WIDTH), tail_map)] * 2 if tail_rows else []
    tail_shapes = [jax.ShapeDtypeStruct((tail_rows, ATT_WIDTH), F32)] * 2 if tail_rows else []
    assert not casts or nt * 6 >= CAST_SLABS
    slab = lambda i, j: (jnp.minimum(i * 6 + j, CAST_SLABS - 1), 0)
    cast_specs = [pl.BlockSpec((a.shape[0] // CAST_SLABS, a.shape[1]), slab) for a in casts]
    cast_shapes = [jax.ShapeDtypeStruct(a.shape, BF16) for a in casts]
    assert all(a.shape[0] % (16 * CAST_SLABS) == 0 for a in casts)
    if grouped:
        past_spec = pl.BlockSpec((tm // 8, CONV_K - 1, CONV_WIDTH), lambda i, j: (i, 0, 0))
        u_shape = jax.ShapeDtypeStruct((m, CONV_WIDTH), F32)
        u_spec = pl.BlockSpec((tm, CONV_WIDTH), row)
    else:
        past_spec = pl.BlockSpec((CONV_K - 1, CONV_WIDTH), const2)
        u_shape = jax.ShapeDtypeStruct((8, CONV_WIDTH), F32)
        u_spec = pl.BlockSpec((8, CONV_WIDTH), const2)
    out_f32 = jax.ShapeDtypeStruct((m, ATT_WIDTH), F32)
    return pl.pallas_call(
        functools.partial(_inproj_kernel, tm=tm, grouped=grouped, n_cast=len(casts),
                          n_tail=len(tail_specs)),
        grid=(nt, 6),
        in_specs=[
            pl.BlockSpec((tm, D_MODEL), row),
            pl.BlockSpec((1, D_MODEL), const2),
            pl.BlockSpec((D_MODEL, 1024), lambda i, j: (0, j)),
            pl.BlockSpec((1, ATT_WIDTH), const2),
            pl.BlockSpec((1, ATT_WIDTH), const2),
            pl.BlockSpec((ATT_WIDTH, LANES), const2),
            pl.BlockSpec((2 * LANES, ATT_WIDTH), const2),
            pl.BlockSpec((CONV_K, CONV_WIDTH), const2),
            past_spec,
        ] + cast_specs,
        out_specs=[
            pl.BlockSpec((tm, ATT_WIDTH), row),
            pl.BlockSpec((tm, ATT_WIDTH), row),
            pl.BlockSpec((tm, ATT_WIDTH), row),
            pl.BlockSpec((tm, CONV_WIDTH), row),
            u_spec,
        ] + cast_specs + tail_specs,
        out_shape=[out_f32, out_f32, out_f32,
                   jax.ShapeDtypeStruct((m, CONV_WIDTH), BF16), u_shape] + cast_shapes + tail_shapes,
        scratch_shapes=[
            pltpu.VMEM((tm, D_MODEL), BF16),
            pltpu.VMEM((tm, CONV_WIDTH), F32),
            pltpu.VMEM((tm, CONV_WIDTH), F32),
            pltpu.VMEM((tm + 8, CONV_WIDTH), F32),
        ],
        compiler_params=pltpu.CompilerParams(
            dimension_semantics=("arbitrary", "arbitrary"),
            vmem_limit_bytes=VMEM_LIMIT_HOST if tail_rows else VMEM_LIMIT),
        name="inproj_grouped" if grouped else "inproj_seq",
    )(x, g1, w_in, qg, kg, red, expand, cw, past, *casts)


def _prompt_bias_table():
    qi = np.arange(Q_TILE)[:, None]
    kc = np.arange(2 * Q_TILE)[None, :]
    jdist = (Q_TILE + qi - kc).astype(np.float64)
    valid = (jdist >= 0) & (jdist <= Q_TILE)
    slopes = _alibi_slopes()
    tab = np.empty((2, len(DILATIONS), N_HEADS, Q_TILE, 2 * Q_TILE), np.float32)
    for f in range(2):
        ok = valid & (kc >= Q_TILE) if f else valid
        for p, d in enumerate(DILATIONS):
            for h in range(N_HEADS):
                tab[f, p, h] = np.where(ok, -slopes[h] * d * jdist * LOG2E, -np.inf)
    return tab


def _prompt_attn_kernel(q_ref, kc_ref, vc_ref, bias_ref,
                        dq_ref, dkn_ref, dvn_ref, dkt_hbm, dvt_hbm, dbias_ref, dbiasn_ref,
                        o_ref, do_ref, o_scr, m_scr, l_scr, kp_ref, vp_ref,
                        dkbuf, dvbuf, dksem, dvsem, *, batch0):
    first = jnp.where(pl.program_id(1) == 0, 1, 0)
    quarter = pl.program_id(2)
    step = (pl.program_id(0) * pl.num_programs(1) + pl.program_id(1)) * pl.num_programs(2) + quarter
    nsteps = pl.num_programs(0) * pl.num_programs(1) * pl.num_programs(2)
    dkt_ref, dvt_ref = _WindowRing(dkt_hbm, dvt_hbm, dkbuf, dvbuf, dksem, dvsem, batch0).advance(
        step, nsteps)

    @pl.when((pl.program_id(1) == 0) & (quarter == 0))
    def _():
        kp_ref[...] = jnp.zeros(kp_ref.shape, F32)
        vp_ref[...] = jnp.zeros(vp_ref.shape, F32)
    lane = lax.broadcasted_iota(jnp.int32, (Q_TILE, LANES), 1)
    lo = lane < HEAD_DIM
    hi = jnp.logical_not(lo)

    def attend(units):
        def scores(unit):
            pat, variant, rows, kprev_ref, vprev_ref, prev = unit
            q = q_ref[rows, :]
            k2 = jnp.concatenate([kprev_ref[prev, :], kc_ref[rows, :]], axis=0).astype(BF16)
            q2 = jnp.concatenate([jnp.where(lo, q, 0.0), jnp.where(hi, q, 0.0)], axis=0).astype(BF16)
            return _dot_nt(q2, k2) + bias_ref[variant, pat].reshape(2 * Q_TILE, 2 * Q_TILE)

        def softmax(s):
            m = jnp.max(s, axis=1, keepdims=True)
            return m, jnp.exp2(s - m).astype(BF16)

        def output(unit, stats):
            pat, _, rows, _, vprev_ref, prev = unit
            v2 = jnp.concatenate([vprev_ref[prev, :], vc_ref[rows, :]], axis=0).astype(BF16)
            v_ext = jnp.concatenate([v2, jnp.ones_like(v2)], axis=1)
            m, p = stats
            f = _dot(p, v_ext)
            o_scr[pat, rows, :] = jnp.where(lo, f[:Q_TILE, :LANES], f[Q_TILE:, :LANES])
            m_scr[pat, rows, :] = jnp.where(lo, m[:Q_TILE], m[Q_TILE:])
            l_scr[pat, rows, :] = jnp.where(lo, f[:Q_TILE, LANES:], f[Q_TILE:, LANES:])

        all_scores = [scores(unit) for unit in units]
        all_stats = [softmax(ss) for ss in all_scores]
        for unit, stats in zip(units, all_stats):
            output(unit, stats)

    def unit16(r):
        rows = pl.ds(r, Q_TILE, stride=16)
        return (2, first, rows, kp_ref, vp_ref, rows)

    def unit4(start, from_prev_block):
        rows = pl.ds(start, Q_TILE, stride=4)
        if from_prev_block:
            return (1, first, rows, kp_ref, vp_ref, pl.ds(SUPER - 512 + start, Q_TILE, stride=4))
        return (1, 0, rows, kc_ref, vc_ref, pl.ds(start - 512, Q_TILE, stride=4))

    def unit1(start, from_prev_block):
        rows = pl.ds(start, Q_TILE)
        if from_prev_block:
            return (0, first, rows, kp_ref, vp_ref, pl.ds(SUPER - Q_TILE, Q_TILE))
        return (0, 0, rows, kc_ref, vc_ref, pl.ds(start - Q_TILE, Q_TILE))

    def pat16(g):
        return [unit16(g * 4 + u) for u in range(4)]

    first4 = [unit4(r4, True) for r4 in range(4)]
    rest4 = [unit4(512 * c + r4, False) for c in range(1, SUPER // 512) for r4 in range(4)]

    def pat1(g):
        return [unit1((1 + 5 * g + u) * Q_TILE, False) for u in range(5)]

    def merge(c, carry):
        rows = pl.ds(pl.multiple_of(c * 256, 256), 256)
        ms = [m_scr[p, rows, :] for p in range(3)]
        mx = jnp.maximum(jnp.maximum(ms[0], ms[1]), ms[2])
        num = jnp.zeros((256, LANES), F32)
        den = jnp.zeros((256, LANES), F32)
        for p in range(3):
            w = jnp.exp2(ms[p] - mx)
            num = num + w * o_scr[p, rows, :]
            den = den + w * l_scr[p, rows, :]
        o_ref[rows, :] = (num / den).astype(BF16)
        return carry

    head_base = (step % (N_HEADS // DEC_HEAD_GROUP)) * DEC_HEAD_GROUP

    def quarter_body(groups):
        dec = _DecodeGroup(head_base, dq_ref, dkn_ref, dvn_ref, dkt_ref, dvt_ref, dbias_ref, dbiasn_ref,
                           do_ref)
        dec.scores()
        attend(groups[0])
        dec.outputs()
        for g in groups[1:]:
            attend(g)

    @pl.when(quarter == 0)
    def _():
        quarter_body([pat16(0), pat16(1), pat1(0)])

    @pl.when(quarter == 1)
    def _():
        quarter_body([pat16(2), pat16(3), pat1(1)])

    @pl.when(quarter == 2)
    def _():
        quarter_body([first4 + [unit1(0, True)], rest4[:6], rest4[6:]])

    @pl.when(quarter == 3)
    def _():
        quarter_body([pat1(2)])
        lax.fori_loop(0, SUPER // 256, merge, 0)
        kp_ref[...] = kc_ref[...]
        vp_ref[...] = vc_ref[...]


PROMPT_QUARTERS = 4


def _prompt_attention(q, k, v, decode, batch0, nbatch):
    s = q.shape[0]
    assert s % SUPER == 0
    bias = jnp.asarray(_prompt_bias_table())
    cur = lambda p, b, c: (b, p)
    blk = (SUPER, LANES)
    nblk = s // SUPER
    grid = (ATT_WIDTH // LANES, nblk, PROMPT_QUARTERS)
    step_of = lambda p, b, c: (p * nblk + b) * PROMPT_QUARTERS + c
    d_specs, d_operands, d_out_spec, d_out_shape = _decode_host_specs(
        decode, batch0, nbatch, grid[0] * grid[1] * grid[2], step_of, manual_windows=True)
    window = (WINDOW_SLOTS, DEC_HEAD_GROUP, HEAD_DIM, decode[3].shape[-1])
    return pl.pallas_call(
        functools.partial(_prompt_attn_kernel, batch0=batch0),
        grid=grid,
        in_specs=[
            pl.BlockSpec(blk, cur),
            pl.BlockSpec(blk, cur),
            pl.BlockSpec(blk, cur),
            pl.BlockSpec((2, 3, 2, Q_TILE, 2 * Q_TILE), lambda p, b, c: (0, 0, p, 0, 0)),
        ] + d_specs,
        out_specs=[pl.BlockSpec(blk, cur), d_out_spec],
        out_shape=[jax.ShapeDtypeStruct((s, ATT_WIDTH), BF16), d_out_shape],
        scratch_shapes=([pltpu.VMEM((3, SUPER, LANES), F32)] * 3 + [pltpu.VMEM(blk, F32)] * 2
                        + [pltpu.VMEM(window, F32)] * 2
                        + [pltpu.SemaphoreType.DMA((WINDOW_SLOTS,))] * 2),
        compiler_params=pltpu.CompilerParams(
            dimension_semantics=("arbitrary", "arbitrary", "arbitrary"), vmem_limit_bytes=VMEM_LIMIT),
        name="prompt_attention_with_decode_attention",
    )(q, k, v, bias, *d_operands)


DEC_SEQ = 8
DEC_HEAD_GROUP = 8


def _decode_bias_tables(past_len):
    dist_max = MAX_WINDOW
    mult = np.zeros(dist_max + 1, np.float64)
    for w, d in zip(WINDOWS, DILATIONS):
        mult[np.arange(w // d + 1) * d] += 1.0
    slopes = _alibi_slopes()

    def table(dist):
        ok = (dist >= 0) & (dist <= dist_max)
        dc = np.clip(dist, 0, dist_max)
        mu = np.where(ok, mult[dc], 0.0)
        with np.errstate(divide="ignore"):
            logm = np.where(mu > 0, np.log(np.maximum(mu, 1e-30)), -np.inf)
        return ((-slopes[:, None, None] * dc[None].astype(np.float64) + logm[None]) * LOG2E).astype(np.float32)

    qpos = past_len + np.arange(DEC_SEQ)
    old = table(qpos[:, None] - np.arange(past_len)[None, :])
    new = np.full((N_HEADS, DEC_SEQ, 2 * DEC_SEQ), -np.inf, np.float32)
    new[:, :, :DEC_SEQ] = table(qpos[:, None] - qpos[None, :])
    return old, new


def _pad16(a):
    return jnp.concatenate([a, jnp.zeros_like(a)], axis=0).astype(BF16)


class _DecodeGroup:
    def __init__(self, head_base, q_ref, kn_ref, vn_ref, kt_ref, vt_ref, bias_ref, biasn_ref, o_ref):
        self.head_base = head_base
        self.refs = (q_ref, kn_ref, vn_ref, kt_ref, vt_ref, bias_ref, biasn_ref, o_ref)

    def scores(self):
        q_ref, kn_ref, _, kt_ref, _, bias_ref, biasn_ref, _ = self.refs
        self.s = []
        for u in range(DEC_HEAD_GROUP):
            c = slice(u * HEAD_DIM, (u + 1) * HEAD_DIM)
            q = _pad16(q_ref[:, c])
            s = _dot(q, kt_ref[u].astype(BF16))[:DEC_SEQ] + bias_ref[self.head_base + u]
            sn = _dot_nt(q, _pad16(kn_ref[:, c]))[:DEC_SEQ] + biasn_ref[self.head_base + u]
            self.s.append((s, sn))

    def outputs(self):
        _, _, vn_ref, _, vt_ref, _, _, o_ref = self.refs
        probs = []
        for s, sn in self.s:
            m = jnp.maximum(jnp.max(s, axis=1, keepdims=True), jnp.max(sn, axis=1, keepdims=True))
            probs.append((jnp.exp2(s - m), jnp.exp2(sn - m)))
        for u, (p, pn) in enumerate(probs):
            c = slice(u * HEAD_DIM, (u + 1) * HEAD_DIM)
            l = jnp.sum(p, axis=1, keepdims=True) + jnp.sum(pn, axis=1, keepdims=True)
            o = (_dot_nt(_pad16(p), vt_ref[u].astype(BF16))[:DEC_SEQ]
                 + _dot(_pad16(pn), _pad16(vn_ref[:, c]))[:DEC_SEQ])
            o_ref[:, c] = o / l


WINDOW_SLOTS = 4


class _WindowRing:
    def __init__(self, kt_hbm, vt_hbm, kbuf, vbuf, ksem, vsem, batch0):
        self.arrays = ((kt_hbm, kbuf, ksem), (vt_hbm, vbuf, vsem))
        self.batch0 = batch0

    def _copies(self, unit):
        gpr = N_HEADS // DEC_HEAD_GROUP
        batch = self.batch0 + unit // gpr
        heads = pl.ds((unit % gpr) * DEC_HEAD_GROUP, DEC_HEAD_GROUP)
        slot = unit % WINDOW_SLOTS
        return [pltpu.make_async_copy(hbm.at[batch, heads], buf.at[slot], sem.at[slot])
                for hbm, buf, sem in self.arrays]

    def advance(self, step, nsteps):
        @pl.when(step == 0)
        def _():
            for ahead in range(WINDOW_SLOTS - 1):
                for c in self._copies(ahead):
                    c.start()

        @pl.when(step + WINDOW_SLOTS - 1 < nsteps)
        def _():
            for c in self._copies(step + WINDOW_SLOTS - 1):
                c.start()

        for c in self._copies(step):
            c.wait()
        slot = step % WINDOW_SLOTS
        return self.arrays[0][1].at[slot], self.arrays[1][1].at[slot]


def _decode_host_specs(decode, batch0, nbatch, nsteps, step_of, manual_windows=False):
    q, kn, vn, kt, vt = decode
    nb, _, _, past = kt.shape
    gpr = N_HEADS // DEC_HEAD_GROUP
    gw = DEC_HEAD_GROUP * HEAD_DIM
    assert q.shape == (nb, DEC_SEQ, ATT_WIDTH) and nsteps == nbatch * gpr and batch0 + nbatch <= nb
    old, new = _decode_bias_tables(past)
    small = pl.BlockSpec((None, DEC_SEQ, gw),
                         lambda *g: (batch0 + step_of(*g) // gpr, 0, step_of(*g) % gpr))
    if manual_windows:
        big = pl.BlockSpec(memory_space=pl.ANY)
    else:
        big = pl.BlockSpec((None, DEC_HEAD_GROUP, HEAD_DIM, past),
                           lambda *g: (batch0 + step_of(*g) // gpr, step_of(*g) % gpr, 0, 0))
    const3 = lambda *g: (0, 0, 0)
    in_specs = [small, small, small, big, big,
                pl.BlockSpec((N_HEADS, DEC_SEQ, past), const3),
                pl.BlockSpec((N_HEADS, DEC_SEQ, 2 * DEC_SEQ), const3)]
    out_spec = pl.BlockSpec((None, DEC_SEQ, gw), lambda *g: (step_of(*g) // gpr, 0, step_of(*g) % gpr))
    out_shape = jax.ShapeDtypeStruct((nbatch, DEC_SEQ, ATT_WIDTH), F32)
    return in_specs, (q, kn, vn, kt, vt, jnp.asarray(old), jnp.asarray(new)), out_spec, out_shape


def _outproj_kernel(x_ref, att_ref, cz_ref, wo_ref, g2_ref, x1_ref, h2_ref):
    mixed = jnp.concatenate([att_ref[...].astype(BF16), cz_ref[...]], axis=1)
    y = x_ref[...] + _dot(mixed, wo_ref[...])
    x1_ref[...] = y
    ms = jnp.mean(y * y, axis=-1, keepdims=True)
    h2_ref[...] = (y * lax.rsqrt(ms + EPS) * g2_ref[...]).astype(BF16)


def _outproj(x, att, cz, w_out, g2, *, tm=OUTPROJ_ROWS):
    m = x.shape[0]
    assert m % tm == 0
    row = lambda i: (i, 0)
    const = lambda i: (0, 0)
    return pl.pallas_call(
        _outproj_kernel,
        grid=(m // tm,),
        in_specs=[
            pl.BlockSpec((tm, D_MODEL), row),
            pl.BlockSpec((tm, ATT_WIDTH), row),
            pl.BlockSpec((tm, CONV_WIDTH), row),
            pl.BlockSpec((D_MODEL, D_MODEL), const),
            pl.BlockSpec((1, D_MODEL), const),
        ],
        out_specs=[pl.BlockSpec((tm, D_MODEL), row), pl.BlockSpec((tm, D_MODEL), row)],
        out_shape=[jax.ShapeDtypeStruct((m, D_MODEL), F32), jax.ShapeDtypeStruct((m, D_MODEL), BF16)],
        compiler_params=pltpu.CompilerParams(
            dimension_semantics=("arbitrary",), vmem_limit_bytes=VMEM_LIMIT),
        name="outproj",
    )(x, att, cz, w_out, g2)


def _ffn_kernel(x1_ref, h2_ref, wu_ref, wd_ref, *rest, groups_per_row):
    y_ref = rest[-2] if groups_per_row else rest[-1]

    @pl.when(pl.program_id(1) == 0)
    def _():
        y_ref[...] = x1_ref[...]

    def ffn_rows(rows):
        a = _dot(h2_ref[rows, :], wu_ref[...])
        g = jnp.square(jnp.maximum(a, 0.0)).astype(BF16)
        y_ref[rows, :] += _dot(g, wd_ref[...])

    if not groups_per_row:
        ffn_rows(slice(None))
        return

    q_ref, kn_ref, vn_ref, kt_ref, vt_ref, bias_ref, biasn_ref, _, o_ref = rest
    step = pl.program_id(0) * pl.num_programs(1) + pl.program_id(1)
    head_base = (step % groups_per_row) * DEC_HEAD_GROUP
    dec = _DecodeGroup(head_base, q_ref, kn_ref, vn_ref, kt_ref, vt_ref, bias_ref, biasn_ref, o_ref)
    half = h2_ref.shape[0] // 2
    dec.scores()
    ffn_rows(slice(0, half))
    dec.outputs()
    ffn_rows(slice(half, 2 * half))


def _ffn(x1, h2, w_up, w_down, decode=None, batch0=0, nbatch=0, *, tm=FFN_ROWS, tf=FFN_COLS):
    m = x1.shape[0]
    assert m % tm == 0 and D_FF % tf == 0
    nf = D_FF // tf
    row = lambda i, f: (i, 0)
    once = dict(pipeline_mode=pl.Buffered(1)) if decode is not None else {}
    in_specs = [
        pl.BlockSpec((tm, D_MODEL), row, **once),
        pl.BlockSpec((tm, D_MODEL), row, **once),
        pl.BlockSpec((D_MODEL, tf), lambda i, f: (0, f)),
        pl.BlockSpec((tf, D_MODEL), lambda i, f: (f, 0)),
    ]
    y_spec = pl.BlockSpec((tm, D_MODEL), row)
    y_shape = jax.ShapeDtypeStruct((m, D_MODEL), F32)
    if decode is None:
        return pl.pallas_call(
            functools.partial(_ffn_kernel, groups_per_row=0),
            grid=(m // tm, nf), in_specs=in_specs, out_specs=y_spec, out_shape=y_shape,
            compiler_params=pltpu.CompilerParams(
                dimension_semantics=("arbitrary", "arbitrary"), vmem_limit_bytes=VMEM_LIMIT),
            name="ffn",
        )(x1, h2, w_up, w_down)

    d_specs, d_operands, d_out_spec, d_out_shape = _decode_host_specs(
        decode, batch0, nbatch, (m // tm) * nf, lambda i, f: i * nf + f)
    return pl.pallas_call(
        functools.partial(_ffn_kernel, groups_per_row=N_HEADS // DEC_HEAD_GROUP),
        grid=(m // tm, nf),
        in_specs=in_specs + d_specs,
        out_specs=[y_spec, d_out_spec],
        out_shape=[y_shape, d_out_shape],
        compiler_params=pltpu.CompilerParams(
            dimension_semantics=("arbitrary", "arbitrary"), vmem_limit_bytes=VMEM_LIMIT_HOST),
        name="ffn_with_decode_attention",
    )(x1, h2, w_up, w_down, *d_operands)


def _batch_minor_kernel(k_ref, v_ref, kt_ref, vt_ref, *, t):
    for src, dst in ((k_ref, kt_ref), (v_ref, vt_ref)):
        x3 = src[...].reshape(src.shape[0] // t, t, ATT_WIDTH)
        for tok in range(t):
            dst[tok] = x3[:, tok, :].T


def _batch_minor(k, v, nb, t):
    full = pl.BlockSpec((nb * t, ATT_WIDTH), lambda i: (0, 0))
    out = pl.BlockSpec((t, ATT_WIDTH, nb), lambda i: (0, 0, 0))
    shape = jax.ShapeDtypeStruct((t, ATT_WIDTH, nb), F32)
    return pl.pallas_call(
        functools.partial(_batch_minor_kernel, t=t),
        grid=(1,), in_specs=[full, full], out_specs=[out, out], out_shape=[shape, shape],
        compiler_params=pltpu.CompilerParams(
            dimension_semantics=("arbitrary",), vmem_limit_bytes=VMEM_LIMIT),
        name="batch_minor_rows",
    )(k, v)


def _head_sum_matrices():
    head_of_col = np.arange(ATT_WIDTH) // HEAD_DIM
    red = (head_of_col[:, None] == np.arange(LANES)[None, :]).astype(np.float32)
    return jnp.asarray(red, BF16), jnp.asarray(np.concatenate([red.T, red.T], axis=0), BF16)


def kernel(x_prompt, x_sample, state_k, state_v, state_conv, norm1_g, w_in, q_norm_g, k_norm_g,
           conv_w, w_out, norm2_g, w_up, w_down):
    depth = w_in.shape[0]
    assert depth == 1
    bp, sp, _ = x_prompt.shape
    nb, t, _ = x_sample.shape
    past = state_k.shape[2]
    assert bp == 1 and t == 8 and past == MAX_WINDOW and sp >= MAX_WINDOW

    red, expand = _head_sum_matrices()
    g1 = norm1_g[0][None]
    g2 = norm2_g[0][None]
    qg = jnp.tile(q_norm_g[0], N_HEADS)[None]
    kg = jnp.tile(k_norm_g[0], N_HEADS)[None]
    cw = conv_w[0]
    w_in_b = w_in[0].astype(BF16)

    xp = x_prompt[0]
    zero_conv = jnp.zeros((CONV_K - 1, CONV_WIDTH), F32)
    keep = min(MAX_WINDOW, sp)
    qp, kp, vp, czp, up, w_out_b, w_up_b, w_down_b, k_tail, v_tail = _inproj(
        xp, zero_conv, g1, w_in_b, qg, kg, red, expand, cw, grouped=False,
        casts=(w_out[0], w_up[0], w_down[0]), tail_rows=keep)

    xs = x_sample.reshape(nb * t, D_MODEL)
    qs, ks, vs, czs, us = _inproj(xs, state_conv[0], g1, w_in_b, qg, kg, red, expand, cw, grouped=True)
    kt = jnp.transpose(state_k[0], (0, 2, 3, 1))
    vt = jnp.transpose(state_v[0], (0, 2, 3, 1))
    per_batch = lambda a: a.reshape(nb, t, ATT_WIDTH)
    decode = (per_batch(qs), per_batch(ks), per_batch(vs), kt, vt)
    gpr = N_HEADS // DEC_HEAD_GROUP
    nb_attn = (sp // SUPER) * (ATT_WIDTH // LANES) * PROMPT_QUARTERS // gpr
    assert 0 < nb_attn < nb

    attp, atts_a = _prompt_attention(qp, kp, vp, decode, 0, nb_attn)
    x1p, h2p = _outproj(xp, attp, czp, w_out_b, g2)
    yp, atts_f = _ffn(x1p, h2p, w_up_b, w_down_b, decode, nb_attn, nb - nb_attn)
    atts = jnp.concatenate([atts_a, atts_f], axis=0).reshape(nb * t, ATT_WIDTH)
    x1s, h2s = _outproj(xs, atts, czs, w_out_b, g2)
    ys = _ffn(x1s, h2s, w_up_b, w_down_b)

    new_k_prompt = k_tail.reshape(1, 1, keep, N_HEADS, HEAD_DIM)
    new_v_prompt = v_tail.reshape(1, 1, keep, N_HEADS, HEAD_DIM)
    new_conv_prompt = up[8 - (CONV_K - 1):].reshape(1, 1, CONV_K - 1, CONV_WIDTH)
    ks_bm, vs_bm = _batch_minor(ks, vs, nb, t)
    to_rows = lambda a: jnp.transpose(a.reshape(1, t, N_HEADS, HEAD_DIM, nb), (0, 4, 1, 2, 3))
    new_k_sample = to_rows(ks_bm)
    new_v_sample = to_rows(vs_bm)
    new_conv_sample = us.reshape(nb, t, CONV_WIDTH)[:, t - (CONV_K - 1):][None]
    return (yp[None], ys.reshape(nb, t, D_MODEL), new_k_prompt, new_v_prompt, new_conv_prompt,
            new_k_sample, new_v_sample, new_conv_sample)
```

```python
import functools

import numpy as np
import jax
import jax.numpy as jnp
from jax import lax
from jax.experimental import pallas as pl
from jax.experimental.pallas import tpu as pltpu

F32 = jnp.float32
BF16 = jnp.bfloat16

D_MODEL = 2048
ATT_WIDTH = 1024
CONV_WIDTH = 1024
HEAD_DIM = 64
N_HEADS = 16
WINDOWS = (128, 512, 2048)
DILATIONS = (1, 4, 16)
MAX_WINDOW = 2048
CONV_K = 3
D_FF = 4 * D_MODEL
EPS = 1e-6
LOG2E = float(np.log2(np.e))
Q_SCALE = HEAD_DIM ** -0.5 * LOG2E

LANES = 128
Q_TILE = 128
SUPER = MAX_WINDOW
VMEM_LIMIT = 56 * 1024 * 1024
VMEM_LIMIT_HOST = 60 * 1024 * 1024

INPROJ_ROWS = 512
OUTPROJ_ROWS = 512
FFN_ROWS, FFN_COLS = 1024, 512


def _dot(a, b):
    return jnp.dot(a, b, preferred_element_type=F32)


def _dot_nt(a, b):
    return lax.dot_general(a, b, (((1,), (1,)), ((), ())), preferred_element_type=F32)


def _alibi_slopes():
    return 2.0 ** (-8.0 * np.arange(1, N_HEADS + 1, dtype=np.float64) / N_HEADS)


def _head_rmsnorm(p, gain_row, red, expand):
    ss = _dot((p * p).astype(BF16), red)
    inv = lax.rsqrt(ss * (1.0 / HEAD_DIM) + EPS)
    hi = inv.astype(BF16)
    lo = (inv - hi.astype(F32)).astype(BF16)
    inv_e = _dot(jnp.concatenate([hi, lo], axis=1), expand)
    return p * inv_e * gain_row


def _inproj_kernel(*refs, tm, grouped, n_cast, n_tail):
    x_ref, g1_ref, w_ref, qg_ref, kg_ref, red_ref, exp_ref, cw_ref, past_ref = refs[:9]
    cast_in = refs[9:9 + n_cast]
    q_ref, k_ref, v_ref, cz_ref, u_ref = refs[9 + n_cast:14 + n_cast]
    cast_out = refs[14 + n_cast:14 + 2 * n_cast]
    tails = refs[14 + 2 * n_cast:14 + 2 * n_cast + n_tail]
    h_scr, gb_scr, c_scr, ubuf = refs[14 + 2 * n_cast + n_tail:]
    i = pl.program_id(0)
    j = pl.program_id(1)

    @pl.when(j == 0)
    def _():
        x = x_ref[...]
        ms = jnp.mean(x * x, axis=-1, keepdims=True)
        h_scr[...] = (x * lax.rsqrt(ms + EPS) * g1_ref[...]).astype(BF16)

    proj = _dot(h_scr[...], w_ref[...])

    for src, dst in zip(cast_in, cast_out):
        dst[...] = src[...].astype(BF16)

    @pl.when(j == 0)
    def _():
        q_ref[...] = _head_rmsnorm(proj, qg_ref[...], red_ref[...], exp_ref[...]) * Q_SCALE

    @pl.when(j == 1)
    def _():
        k = _head_rmsnorm(proj, kg_ref[...], red_ref[...], exp_ref[...])
        k_ref[...] = k
        if tails:
            tails[0][...] = k

    @pl.when(j == 2)
    def _():
        v_ref[...] = proj
        if tails:
            tails[1][...] = proj

    @pl.when(j == 3)
    def _():
        gb_scr[...] = proj

    @pl.when(j == 4)
    def _():
        c_scr[...] = proj

    @pl.when(j == 5)
    def _():
        u = c_scr[...] * proj
        cw = cw_ref[...]
        if grouped:
            g = tm // 8
            u3 = u.reshape(g, 8, CONV_WIDTH)
            past = past_ref[...]
            p0 = past[:, 0:1, :]
            p1 = past[:, 1:2, :]
            tok = lax.broadcasted_iota(jnp.int32, u3.shape, 1)
            prev1 = jnp.where(tok == 0, p1, pltpu.roll(u3, 1, axis=1))
            prev2 = jnp.where(tok == 0, p0, jnp.where(tok == 1, p1, pltpu.roll(u3, 2, axis=1)))
            conv = cw[0:1][None] * prev2 + cw[1:2][None] * prev1 + cw[2:3][None] * u3
            cz_ref[...] = (gb_scr[...] * conv.reshape(tm, CONV_WIDTH)).astype(BF16)
            u_ref[...] = u
        else:
            @pl.when(i == 0)
            def _():
                ubuf[0:8, :] = jnp.zeros((8, CONV_WIDTH), F32)
                ubuf[6:8, :] = past_ref[...]

            ubuf[8:tm + 8, :] = u
            conv = cw[0:1] * ubuf[6:tm + 6, :] + cw[1:2] * ubuf[7:tm + 7, :] + cw[2:3] * u
            cz_ref[...] = (gb_scr[...] * conv).astype(BF16)
            tail = ubuf[tm:tm + 8, :]
            ubuf[0:8, :] = tail
            u_ref[...] = tail


CAST_SLABS = 64


def _inproj(x, past, g1, w_in, qg, kg, red, expand, cw, *, grouped, casts=(), tail_rows=0,
            tm=INPROJ_ROWS):
    m = x.shape[0]
    assert m % tm == 0 and tail_rows % tm == 0
    nt = m // tm
    row = lambda i, j: (i, 0)
    const2 = lambda i, j: (0, 0)
    tail_map = lambda i, j: (jnp.maximum(i - (nt - tail_rows // tm), 0), 0)
    tail_specs = [pl.BlockSpec((tm, ATT_WIDTH), tail_map)] * 2 if tail_rows else []
    tail_shapes = [jax.ShapeDtypeStruct((tail_rows, ATT_WIDTH), F32)] * 2 if tail_rows else []
    assert not casts or nt * 6 >= CAST_SLABS
    slab = lambda i, j: (jnp.minimum(i * 6 + j, CAST_SLABS - 1), 0)
    cast_specs = [pl.BlockSpec((a.shape[0] // CAST_SLABS, a.shape[1]), slab) for a in casts]
    cast_shapes = [jax.ShapeDtypeStruct(a.shape, BF16) for a in casts]
    assert all(a.shape[0] % (16 * CAST_SLABS) == 0 for a in casts)
    if grouped:
        past_spec = pl.BlockSpec((tm // 8, CONV_K - 1, CONV_WIDTH), lambda i, j: (i, 0, 0))
        u_shape = jax.ShapeDtypeStruct((m, CONV_WIDTH), F32)
        u_spec = pl.BlockSpec((tm, CONV_WIDTH), row)
    else:
        past_spec = pl.BlockSpec((CONV_K - 1, CONV_WIDTH), const2)
        u_shape = jax.ShapeDtypeStruct((8, CONV_WIDTH), F32)
        u_spec = pl.BlockSpec((8, CONV_WIDTH), const2)
    out_f32 = jax.ShapeDtypeStruct((m, ATT_WIDTH), F32)
    return pl.pallas_call(
        functools.partial(_inproj_kernel, tm=tm, grouped=grouped, n_cast=len(casts),
                          n_tail=len(tail_specs)),
        grid=(nt, 6),
        in_specs=[
            pl.BlockSpec((tm, D_MODEL), row),
            pl.BlockSpec((1, D_MODEL), const2),
            pl.BlockSpec((D_MODEL, 1024), lambda i, j: (0, j)),
            pl.BlockSpec((1, ATT_WIDTH), const2),
            pl.BlockSpec((1, ATT_WIDTH), const2),
            pl.BlockSpec((ATT_WIDTH, LANES), const2),
            pl.BlockSpec((2 * LANES, ATT_WIDTH), const2),
            pl.BlockSpec((CONV_K, CONV_WIDTH), const2),
            past_spec,
        ] + cast_specs,
        out_specs=[
            pl.BlockSpec((tm, ATT_WIDTH), row),
            pl.BlockSpec((tm, ATT_WIDTH), row),
            pl.BlockSpec((tm, ATT_WIDTH), row),
            pl.BlockSpec((tm, CONV_WIDTH), row),
            u_spec,
        ] + cast_specs + tail_specs,
        out_shape=[out_f32, out_f32, out_f32,
                   jax.ShapeDtypeStruct((m, CONV_WIDTH), BF16), u_shape] + cast_shapes + tail_shapes,
        scratch_shapes=[
            pltpu.VMEM((tm, D_MODEL), BF16),
            pltpu.VMEM((tm, CONV_WIDTH), F32),
            pltpu.VMEM((tm, CONV_WIDTH), F32),
            pltpu.VMEM((tm + 8, CONV_WIDTH), F32),
        ],
        compiler_params=pltpu.CompilerParams(
            dimension_semantics=("arbitrary", "arbitrary"),
            vmem_limit_bytes=VMEM_LIMIT_HOST if tail_rows else VMEM_LIMIT),
        name="inproj_grouped" if grouped else "inproj_seq",
    )(x, g1, w_in, qg, kg, red, expand, cw, past, *casts)


def _prompt_bias_table():
    qi = np.arange(Q_TILE)[:, None]
    kc = np.arange(2 * Q_TILE)[None, :]
    jdist = (Q_TILE + qi - kc).astype(np.float64)
    valid = (jdist >= 0) & (jdist <= Q_TILE)
    slopes = _alibi_slopes()
    tab = np.empty((2, len(DILATIONS), N_HEADS, Q_TILE, 2 * Q_TILE), np.float32)
    for f in range(2):
        ok = valid & (kc >= Q_TILE) if f else valid
        for p, d in enumerate(DILATIONS):
            for h in range(N_HEADS):
                tab[f, p, h] = np.where(ok, -slopes[h] * d * jdist * LOG2E, -np.inf)
    return tab


def _prompt_attn_kernel(q_ref, kc_ref, vc_ref, bias_ref,
                        dq_ref, dkn_ref, dvn_ref, dkt_hbm, dvt_hbm, dbias_ref, dbiasn_ref,
                        o_ref, do_ref, o_scr, m_scr, l_scr, kp_ref, vp_ref,
                        dkbuf, dvbuf, dksem, dvsem, *, batch0):
    first = jnp.where(pl.program_id(1) == 0, 1, 0)
    quarter = pl.program_id(2)
    step = (pl.program_id(0) * pl.num_programs(1) + pl.program_id(1)) * pl.num_programs(2) + quarter
    nsteps = pl.num_programs(0) * pl.num_programs(1) * pl.num_programs(2)
    dkt_ref, dvt_ref = _WindowRing(dkt_hbm, dvt_hbm, dkbuf, dvbuf, dksem, dvsem, batch0).advance(
        step, nsteps)

    @pl.when((pl.program_id(1) == 0) & (quarter == 0))
    def _():
        kp_ref[...] = jnp.zeros(kp_ref.shape, F32)
        vp_ref[...] = jnp.zeros(vp_ref.shape, F32)
    lane = lax.broadcasted_iota(jnp.int32, (Q_TILE, LANES), 1)
    lo = lane < HEAD_DIM
    hi = jnp.logical_not(lo)

    def attend(units):
        def scores(unit):
            pat, variant, rows, kprev_ref, vprev_ref, prev = unit
            q = q_ref[rows, :]
            k2 = jnp.concatenate([kprev_ref[prev, :], kc_ref[rows, :]], axis=0).astype(BF16)
            q2 = jnp.concatenate([jnp.where(lo, q, 0.0), jnp.where(hi, q, 0.0)], axis=0).astype(BF16)
            return _dot_nt(q2, k2) + bias_ref[variant, pat].reshape(2 * Q_TILE, 2 * Q_TILE)

        def softmax(s):
            m = jnp.max(s, axis=1, keepdims=True)
            return m, jnp.exp2(s - m).astype(BF16)

        def output(unit, stats):
            pat, _, rows, _, vprev_ref, prev = unit
            v2 = jnp.concatenate([vprev_ref[prev, :], vc_ref[rows, :]], axis=0).astype(BF16)
            v_ext = jnp.concatenate([v2, jnp.ones_like(v2)], axis=1)
            m, p = stats
            f = _dot(p, v_ext)
            o_scr[pat, rows, :] = jnp.where(lo, f[:Q_TILE, :LANES], f[Q_TILE:, :LANES])
            m_scr[pat, rows, :] = jnp.where(lo, m[:Q_TILE], m[Q_TILE:])
            l_scr[pat, rows, :] = jnp.where(lo, f[:Q_TILE, LANES:], f[Q_TILE:, LANES:])

        all_scores = [scores(unit) for unit in units]
        all_stats = [softmax(ss) for ss in all_scores]
        for unit, stats in zip(units, all_stats):
            output(unit, stats)

    def unit16(r):
        rows = pl.ds(r, Q_TILE, stride=16)
        return (2, first, rows, kp_ref, vp_ref, rows)

    def unit4(start, from_prev_block):
        rows = pl.ds(start, Q_TILE, stride=4)
        if from_prev_block:
            return (1, first, rows, kp_ref, vp_ref, pl.ds(SUPER - 512 + start, Q_TILE, stride=4))
        return (1, 0, rows, kc_ref, vc_ref, pl.ds(start - 512, Q_TILE, stride=4))

    def unit1(start, from_prev_block):
        rows = pl.ds(start, Q_TILE)
        if from_prev_block:
            return (0, first, rows, kp_ref, vp_ref, pl.ds(SUPER - Q_TILE, Q_TILE))
        return (0, 0, rows, kc_ref, vc_ref, pl.ds(start - Q_TILE, Q_TILE))

    def pat16(g):
        return [unit16(g * 4 + u) for u in range(4)]

    first4 = [unit4(r4, True) for r4 in range(4)]
    rest4 = [unit4(512 * c + r4, False) for c in range(1, SUPER // 512) for r4 in range(4)]

    def pat1(g):
        return [unit1((1 + 5 * g + u) * Q_TILE, False) for u in range(5)]

    def merge(c, carry):
        rows = pl.ds(pl.multiple_of(c * 256, 256), 256)
        ms = [m_scr[p, rows, :] for p in range(3)]
        mx = jnp.maximum(jnp.maximum(ms[0], ms[1]), ms[2])
        num = jnp.zeros((256, LANES), F32)
        den = jnp.zeros((256, LANES), F32)
        for p in range(3):
            w = jnp.exp2(ms[p] - mx)
            num = num + w * o_scr[p, rows, :]
            den = den + w * l_scr[p, rows, :]
        o_ref[rows, :] = (num / den).astype(BF16)
        return carry

    head_base = (step % (N_HEADS // DEC_HEAD_GROUP)) * DEC_HEAD_GROUP

    def quarter_body(groups):
        dec = _DecodeGroup(head_base, dq_ref, dkn_ref, dvn_ref, dkt_ref, dvt_ref, dbias_ref, dbiasn_ref,
                           do_ref)
        dec.scores()
        attend(groups[0])
        dec.outputs()
        for g in groups[1:]:
            attend(g)

    @pl.when(quarter == 0)
    def _():
        quarter_body([pat16(0), pat16(1), pat1(0)])

    @pl.when(quarter == 1)
    def _():
        quarter_body([pat16(2), pat16(3), pat1(1)])

    @pl.when(quarter == 2)
    def _():
        quarter_body([first4 + [unit1(0, True)], rest4[:6], rest4[6:]])

    @pl.when(quarter == 3)
    def _():
        quarter_body([pat1(2)])
        lax.fori_loop(0, SUPER // 256, merge, 0)
        kp_ref[...] = kc_ref[...]
        vp_ref[...] = vc_ref[...]


PROMPT_QUARTERS = 4


def _prompt_attention(q, k, v, decode, batch0, nbatch):
    s = q.shape[0]
    assert s % SUPER == 0
    bias = jnp.asarray(_prompt_bias_table())
    cur = lambda p, b, c: (b, p)
    blk = (SUPER, LANES)
    nblk = s // SUPER
    grid = (ATT_WIDTH // LANES, nblk, PROMPT_QUARTERS)
    step_of = lambda p, b, c: (p * nblk + b) * PROMPT_QUARTERS + c
    d_specs, d_operands, d_out_spec, d_out_shape = _decode_host_specs(
        decode, batch0, nbatch, grid[0] * grid[1] * grid[2], step_of, manual_windows=True)
    window = (WINDOW_SLOTS, DEC_HEAD_GROUP, HEAD_DIM, decode[3].shape[-1])
    return pl.pallas_call(
        functools.partial(_prompt_attn_kernel, batch0=batch0),
        grid=grid,
        in_specs=[
            pl.BlockSpec(blk, cur),
            pl.BlockSpec(blk, cur),
            pl.BlockSpec(blk, cur),
            pl.BlockSpec((2, 3, 2, Q_TILE, 2 * Q_TILE), lambda p, b, c: (0, 0, p, 0, 0)),
        ] + d_specs,
        out_specs=[pl.BlockSpec(blk, cur), d_out_spec],
        out_shape=[jax.ShapeDtypeStruct((s, ATT_WIDTH), BF16), d_out_shape],
        scratch_shapes=([pltpu.VMEM((3, SUPER, LANES), F32)] * 3 + [pltpu.VMEM(blk, F32)] * 2
                        + [pltpu.VMEM(window, F32)] * 2
                        + [pltpu.SemaphoreType.DMA((WINDOW_SLOTS,))] * 2),
        compiler_params=pltpu.CompilerParams(
            dimension_semantics=("arbitrary", "arbitrary", "arbitrary"), vmem_limit_bytes=VMEM_LIMIT),
        name="prompt_attention_with_decode_attention",
    )(q, k, v, bias, *d_operands)


DEC_SEQ = 8
DEC_HEAD_GROUP = 8


def _decode_bias_tables(past_len):
    dist_max = MAX_WINDOW
    mult = np.zeros(dist_max + 1, np.float64)
    for w, d in zip(WINDOWS, DILATIONS):
        mult[np.arange(w // d + 1) * d] += 1.0
    slopes = _alibi_slopes()

    def table(dist):
        ok = (dist >= 0) & (dist <= dist_max)
        dc = np.clip(dist, 0, dist_max)
        mu = np.where(ok, mult[dc], 0.0)
        with np.errstate(divide="ignore"):
            logm = np.where(mu > 0, np.log(np.maximum(mu, 1e-30)), -np.inf)
        return ((-slopes[:, None, None] * dc[None].astype(np.float64) + logm[None]) * LOG2E).astype(np.float32)

    qpos = past_len + np.arange(DEC_SEQ)
    old = table(qpos[:, None] - np.arange(past_len)[None, :])
    new = np.full((N_HEADS, DEC_SEQ, 2 * DEC_SEQ), -np.inf, np.float32)
    new[:, :, :DEC_SEQ] = table(qpos[:, None] - qpos[None, :])
    return old, new


def _pad16(a):
    return jnp.concatenate([a, jnp.zeros_like(a)], axis=0).astype(BF16)


class _DecodeGroup:
    def __init__(self, head_base, q_ref, kn_ref, vn_ref, kt_ref, vt_ref, bias_ref, biasn_ref, o_ref):
        self.head_base = head_base
        self.refs = (q_ref, kn_ref, vn_ref, kt_ref, vt_ref, bias_ref, biasn_ref, o_ref)

    def scores(self):
        q_ref, kn_ref, _, kt_ref, _, bias_ref, biasn_ref, _ = self.refs
        self.s = []
        for u in range(DEC_HEAD_GROUP):
            c = slice(u * HEAD_DIM, (u + 1) * HEAD_DIM)
            q = _pad16(q_ref[:, c])
            s = _dot(q, kt_ref[u].astype(BF16))[:DEC_SEQ] + bias_ref[self.head_base + u]
            sn = _dot_nt(q, _pad16(kn_ref[:, c]))[:DEC_SEQ] + biasn_ref[self.head_base + u]
            self.s.append((s, sn))

    def outputs(self):
        _, _, vn_ref, _, vt_ref, _, _, o_ref = self.refs
        probs = []
        for s, sn in self.s:
            m = jnp.maximum(jnp.max(s, axis=1, keepdims=True), jnp.max(sn, axis=1, keepdims=True))
            probs.append((jnp.exp2(s - m), jnp.exp2(sn - m)))
        for u, (p, pn) in enumerate(probs):
            c = slice(u * HEAD_DIM, (u + 1) * HEAD_DIM)
            l = jnp.sum(p, axis=1, keepdims=True) + jnp.sum(pn, axis=1, keepdims=True)
            o = (_dot_nt(_pad16(p), vt_ref[u].astype(BF16))[:DEC_SEQ]
                 + _dot(_pad16(pn), _pad16(vn_ref[:, c]))[:DEC_SEQ])
            o_ref[:, c] = o / l


WINDOW_SLOTS = 3
RING_DMA_PRIORITY = 1


class _WindowRing:
    def __init__(self, kt_hbm, vt_hbm, kbuf, vbuf, ksem, vsem, batch0):
        self.arrays = ((kt_hbm, kbuf, ksem), (vt_hbm, vbuf, vsem))
        self.batch0 = batch0

    def _copies(self, unit):
        gpr = N_HEADS // DEC_HEAD_GROUP
        batch = self.batch0 + unit // gpr
        heads = pl.ds((unit % gpr) * DEC_HEAD_GROUP, DEC_HEAD_GROUP)
        slot = unit % WINDOW_SLOTS
        return [pltpu.make_async_copy(hbm.at[batch, heads], buf.at[slot], sem.at[slot])
                for hbm, buf, sem in self.arrays]

    def advance(self, step, nsteps):
        @pl.when(step == 0)
        def _():
            for ahead in range(WINDOW_SLOTS - 1):
                for c in self._copies(ahead):
                    c.start(priority=RING_DMA_PRIORITY)

        @pl.when(step + WINDOW_SLOTS - 1 < nsteps)
        def _():
            for c in self._copies(step + WINDOW_SLOTS - 1):
                c.start(priority=RING_DMA_PRIORITY)

        for c in self._copies(step):
            c.wait()
        slot = step % WINDOW_SLOTS
        return self.arrays[0][1].at[slot], self.arrays[1][1].at[slot]


def _decode_host_specs(decode, batch0, nbatch, nsteps, step_of, manual_windows=False):
    q, kn, vn, kt, vt = decode
    nb, _, _, past = kt.shape
    gpr = N_HEADS // DEC_HEAD_GROUP
    gw = DEC_HEAD_GROUP * HEAD_DIM
    assert q.shape == (nb, DEC_SEQ, ATT_WIDTH) and nsteps == nbatch * gpr and batch0 + nbatch <= nb
    old, new = _decode_bias_tables(past)
    small = pl.BlockSpec((None, DEC_SEQ, gw),
                         lambda *g: (batch0 + step_of(*g) // gpr, 0, step_of(*g) % gpr))
    if manual_windows:
        big = pl.BlockSpec(memory_space=pl.ANY)
    else:
        big = pl.BlockSpec((None, DEC_HEAD_GROUP, HEAD_DIM, past),
                           lambda *g: (batch0 + step_of(*g) // gpr, step_of(*g) % gpr, 0, 0))
    const3 = lambda *g: (0, 0, 0)
    in_specs = [small, small, small, big, big,
                pl.BlockSpec((N_HEADS, DEC_SEQ, past), const3),
                pl.BlockSpec((N_HEADS, DEC_SEQ, 2 * DEC_SEQ), const3)]
    out_spec = pl.BlockSpec((None, DEC_SEQ, gw), lambda *g: (step_of(*g) // gpr, 0, step_of(*g) % gpr))
    out_shape = jax.ShapeDtypeStruct((nbatch, DEC_SEQ, ATT_WIDTH), F32)
    return in_specs, (q, kn, vn, kt, vt, jnp.asarray(old), jnp.asarray(new)), out_spec, out_shape


def _outproj_kernel(x_ref, att_ref, cz_ref, wo_ref, g2_ref, x1_ref, h2_ref):
    mixed = jnp.concatenate([att_ref[...].astype(BF16), cz_ref[...]], axis=1)
    y = x_ref[...] + _dot(mixed, wo_ref[...])
    x1_ref[...] = y
    ms = jnp.mean(y * y, axis=-1, keepdims=True)
    h2_ref[...] = (y * lax.rsqrt(ms + EPS) * g2_ref[...]).astype(BF16)


def _outproj(x, att, cz, w_out, g2, *, tm=OUTPROJ_ROWS):
    m = x.shape[0]
    assert m % tm == 0
    row = lambda i: (i, 0)
    const = lambda i: (0, 0)
    return pl.pallas_call(
        _outproj_kernel,
        grid=(m // tm,),
        in_specs=[
            pl.BlockSpec((tm, D_MODEL), row),
            pl.BlockSpec((tm, ATT_WIDTH), row),
            pl.BlockSpec((tm, CONV_WIDTH), row),
            pl.BlockSpec((D_MODEL, D_MODEL), const),
            pl.BlockSpec((1, D_MODEL), const),
        ],
        out_specs=[pl.BlockSpec((tm, D_MODEL), row), pl.BlockSpec((tm, D_MODEL), row)],
        out_shape=[jax.ShapeDtypeStruct((m, D_MODEL), F32), jax.ShapeDtypeStruct((m, D_MODEL), BF16)],
        compiler_params=pltpu.CompilerParams(
            dimension_semantics=("arbitrary",), vmem_limit_bytes=VMEM_LIMIT),
        name="outproj",
    )(x, att, cz, w_out, g2)


def _ffn_kernel(x1_ref, h2_ref, wu_ref, wd_ref, *rest, groups_per_row):
    y_ref = rest[-2] if groups_per_row else rest[-1]

    @pl.when(pl.program_id(1) == 0)
    def _():
        y_ref[...] = x1_ref[...]

    def ffn_rows(rows):
        a = _dot(h2_ref[rows, :], wu_ref[...])
        g = jnp.square(jnp.maximum(a, 0.0)).astype(BF16)
        y_ref[rows, :] += _dot(g, wd_ref[...])

    if not groups_per_row:
        ffn_rows(slice(None))
        return

    q_ref, kn_ref, vn_ref, kt_ref, vt_ref, bias_ref, biasn_ref, _, o_ref = rest
    step = pl.program_id(0) * pl.num_programs(1) + pl.program_id(1)
    head_base = (step % groups_per_row) * DEC_HEAD_GROUP
    dec = _DecodeGroup(head_base, q_ref, kn_ref, vn_ref, kt_ref, vt_ref, bias_ref, biasn_ref, o_ref)
    half = h2_ref.shape[0] // 2
    dec.scores()
    ffn_rows(slice(0, half))
    dec.outputs()
    ffn_rows(slice(half, 2 * half))


def _ffn(x1, h2, w_up, w_down, decode=None, batch0=0, nbatch=0, *, tm=FFN_ROWS, tf=FFN_COLS):
    m = x1.shape[0]
    assert m % tm == 0 and D_FF % tf == 0
    nf = D_FF // tf
    row = lambda i, f: (i, 0)
    once = dict(pipeline_mode=pl.Buffered(1)) if decode is not None else {}
    in_specs = [
        pl.BlockSpec((tm, D_MODEL), row, **once),
        pl.BlockSpec((tm, D_MODEL), row, **once),
        pl.BlockSpec((D_MODEL, tf), lambda i, f: (0, f)),
        pl.BlockSpec((tf, D_MODEL), lambda i, f: (f, 0)),
    ]
    y_spec = pl.BlockSpec((tm, D_MODEL), row)
    y_shape = jax.ShapeDtypeStruct((m, D_MODEL), F32)
    if decode is None:
        return pl.pallas_call(
            functools.partial(_ffn_kernel, groups_per_row=0),
            grid=(m // tm, nf), in_specs=in_specs, out_specs=y_spec, out_shape=y_shape,
            compiler_params=pltpu.CompilerParams(
                dimension_semantics=("arbitrary", "arbitrary"), vmem_limit_bytes=VMEM_LIMIT),
            name="ffn",
        )(x1, h2, w_up, w_down)

    d_specs, d_operands, d_out_spec, d_out_shape = _decode_host_specs(
        decode, batch0, nbatch, (m // tm) * nf, lambda i, f: i * nf + f)
    return pl.pallas_call(
        functools.partial(_ffn_kernel, groups_per_row=N_HEADS // DEC_HEAD_GROUP),
        grid=(m // tm, nf),
        in_specs=in_specs + d_specs,
        out_specs=[y_spec, d_out_spec],
        out_shape=[y_shape, d_out_shape],
        compiler_params=pltpu.CompilerParams(
            dimension_semantics=("arbitrary", "arbitrary"), vmem_limit_bytes=VMEM_LIMIT_HOST),
        name="ffn_with_decode_attention",
    )(x1, h2, w_up, w_down, *d_operands)


def _batch_minor_kernel(k_ref, v_ref, kt_ref, vt_ref, *, t):
    for src, dst in ((k_ref, kt_ref), (v_ref, vt_ref)):
        x3 = src[...].reshape(src.shape[0] // t, t, ATT_WIDTH)
        for tok in range(t):
            dst[tok] = x3[:, tok, :].T


def _batch_minor(k, v, nb, t):
    full = pl.BlockSpec((nb * t, ATT_WIDTH), lambda i: (0, 0))
    out = pl.BlockSpec((t, ATT_WIDTH, nb), lambda i: (0, 0, 0))
    shape = jax.ShapeDtypeStruct((t, ATT_WIDTH, nb), F32)
    return pl.pallas_call(
        functools.partial(_batch_minor_kernel, t=t),
        grid=(1,), in_specs=[full, full], out_specs=[out, out], out_shape=[shape, shape],
        compiler_params=pltpu.CompilerParams(
            dimension_semantics=("arbitrary",), vmem_limit_bytes=VMEM_LIMIT),
        name="batch_minor_rows",
    )(k, v)


def _head_sum_matrices():
    head_of_col = np.arange(ATT_WIDTH) // HEAD_DIM
    red = (head_of_col[:, None] == np.arange(LANES)[None, :]).astype(np.float32)
    return jnp.asarray(red, BF16), jnp.asarray(np.concatenate([red.T, red.T], axis=0), BF16)


def kernel(x_prompt, x_sample, state_k, state_v, state_conv, norm1_g, w_in, q_norm_g, k_norm_g,
           conv_w, w_out, norm2_g, w_up, w_down):
    depth = w_in.shape[0]
    assert depth == 1
    bp, sp, _ = x_prompt.shape
    nb, t, _ = x_sample.shape
    past = state_k.shape[2]
    assert bp == 1 and t == 8 and past == MAX_WINDOW and sp >= MAX_WINDOW

    red, expand = _head_sum_matrices()
    g1 = norm1_g[0][None]
    g2 = norm2_g[0][None]
    qg = jnp.tile(q_norm_g[0], N_HEADS)[None]
    kg = jnp.tile(k_norm_g[0], N_HEADS)[None]
    cw = conv_w[0]
    w_in_b = w_in[0].astype(BF16)

    xp = x_prompt[0]
    zero_conv = jnp.zeros((CONV_K - 1, CONV_WIDTH), F32)
    keep = min(MAX_WINDOW, sp)
    qp, kp, vp, czp, up, w_out_b, w_up_b, w_down_b, k_tail, v_tail = _inproj(
        xp, zero_conv, g1, w_in_b, qg, kg, red, expand, cw, grouped=False,
        casts=(w_out[0], w_up[0], w_down[0]), tail_rows=keep)

    xs = x_sample.reshape(nb * t, D_MODEL)
    qs, ks, vs, czs, us = _inproj(xs, state_conv[0], g1, w_in_b, qg, kg, red, expand, cw, grouped=True)
    kt = jnp.transpose(state_k[0], (0, 2, 3, 1))
    vt = jnp.transpose(state_v[0], (0, 2, 3, 1))
    per_batch = lambda a: a.reshape(nb, t, ATT_WIDTH)
    decode = (per_batch(qs), per_batch(ks), per_batch(vs), kt, vt)
    gpr = N_HEADS // DEC_HEAD_GROUP
    nb_attn = (sp // SUPER) * (ATT_WIDTH // LANES) * PROMPT_QUARTERS // gpr
    assert 0 < nb_attn < nb

    attp, atts_a = _prompt_attention(qp, kp, vp, decode, 0, nb_attn)
    x1p, h2p = _outproj(xp, attp, czp, w_out_b, g2)
    yp, atts_f = _ffn(x1p, h2p, w_up_b, w_down_b, decode, nb_attn, nb - nb_attn)
    atts = jnp.concatenate([atts_a, atts_f], axis=0).reshape(nb * t, ATT_WIDTH)
    x1s, h2s = _outproj(xs, atts, czs, w_out_b, g2)
    ys = _ffn(x1s, h2s, w_up_b, w_down_b)

    new_k_prompt = k_tail.reshape(1, 1, keep, N_HEADS, HEAD_DIM)
    new_v_prompt = v_tail.reshape(1, 1, keep, N_HEADS, HEAD_DIM)
    new_conv_prompt = up[8 - (CONV_K - 1):].reshape(1, 1, CONV_K - 1, CONV_WIDTH)
    ks_bm, vs_bm = _batch_minor(ks, vs, nb, t)
    to_rows = lambda a: jnp.transpose(a.reshape(1, t, N_HEADS, HEAD_DIM, nb), (0, 4, 1, 2, 3))
    new_k_sample = to_rows(ks_bm)
    new_v_sample = to_rows(vs_bm)
    new_conv_sample = us.reshape(nb, t, CONV_WIDTH)[:, t - (CONV_K - 1):][None]
    return (yp[None], ys.reshape(nb, t, D_MODEL), new_k_prompt, new_v_prompt, new_conv_prompt,
            new_k_sample, new_v_sample, new_conv_sample)
```
